```python
import math
import jax, jax.numpy as jnp
from jax import lax
import numpy as np

D_MODEL = 2048
BATCH = 16
SEQ = 2048
DEPTH = 2

D_MIX = D_MODEL
EPS = 1e-5
POOL_WIDTH = D_MIX // 4
POOL_WINDOWS = (2, 4, 8, 16)
POOL_GROUPS = len(POOL_WINDOWS)
POOL_GROUP_DIM = POOL_WIDTH // POOL_GROUPS
MLA_HEADS = 8
MLA_NOPE_DIM = 128
MLA_ROPE_DIM = 64
MLA_V_DIM = 128
MLA_WIDTH = MLA_HEADS * MLA_V_DIM
MLA_Q_RANK = 512
MLA_KV_RANK = 256
MLA_QK_DIM = MLA_NOPE_DIM + MLA_ROPE_DIM
ROPE_THETA = 10000.0
Q_BLOCK = 128
SGU_WIDTH = D_MIX - POOL_WIDTH - MLA_WIDTH
SGU_HEADS = 4
SGU_HEAD_DIM = SGU_WIDTH // SGU_HEADS
SGU_CHUNK = 128
ALPHA = (2.0 * DEPTH) ** 0.25
BETA = (8.0 * DEPTH) ** -0.25

COL_A_X = POOL_WIDTH
COL_A_G = POOL_WIDTH
COL_B_CQ = MLA_Q_RANK
COL_B_CKV = MLA_KV_RANK
COL_B_KR = MLA_ROPE_DIM
COL_B_G = MLA_WIDTH
COL_C_UV = 2 * SGU_WIDTH
COL_C_G = SGU_WIDTH
D_IN_COLS = COL_A_X + COL_A_G + COL_B_CQ + COL_B_CKV + COL_B_KR + COL_B_G + COL_C_UV + COL_C_G
SPLIT_IDX = (
    COL_A_X,
    COL_A_X + COL_A_G,
    COL_A_X + COL_A_G + COL_B_CQ,
    COL_A_X + COL_A_G + COL_B_CQ + COL_B_CKV,
    COL_A_X + COL_A_G + COL_B_CQ + COL_B_CKV + COL_B_KR,
    COL_A_X + COL_A_G + COL_B_CQ + COL_B_CKV + COL_B_KR + COL_B_G,
    COL_A_X + COL_A_G + COL_B_CQ + COL_B_CKV + COL_B_KR + COL_B_G + COL_C_UV,
)

kernel_name = "hybrid_pool_mla_sgu_deepnorm"


def _layer_norm(x, g, b):
    xf = x.astype(jnp.float32)
    mu = jnp.mean(xf, axis=-1, keepdims=True)
    var = jnp.mean(jnp.square(xf - mu), axis=-1, keepdims=True)
    return ((xf - mu) * lax.rsqrt(var + EPS) * g.astype(jnp.float32) + b.astype(jnp.float32)).astype(x.dtype)


def _rms_norm(x, g):
    xf = x.astype(jnp.float32)
    ms = jnp.mean(jnp.square(xf), axis=-1, keepdims=True)
    return (xf * lax.rsqrt(ms + EPS) * g.astype(jnp.float32)).astype(x.dtype)


def _rope(x, cos, sin):
    half = MLA_ROPE_DIM // 2
    xf = x.astype(jnp.float32)
    x1, x2 = xf[..., :half], xf[..., half:]
    out = jnp.concatenate([x1 * cos - x2 * sin, x2 * cos + x1 * sin], axis=-1)
    return out.astype(x.dtype)


def _pool_mixer(xa, w_pool, pool_scale):
    B, S, _ = xa.shape
    xg = xa.reshape(B, S, POOL_GROUPS, POOL_GROUP_DIM)
    cs = jnp.cumsum(xg.astype(jnp.float32), axis=1)
    t = jnp.arange(1, S + 1, dtype=jnp.float32)[None, :, None]
    pooled = []
    for gi, w in enumerate(POOL_WINDOWS):
        c = cs[:, :, gi]
        prev = jnp.pad(c, ((0, 0), (w, 0), (0, 0)))[:, :S]
        pooled.append((c - prev) / jnp.minimum(t, float(w)))
    pooled = jnp.stack(pooled, axis=2).astype(xa.dtype) - xg
    y = jnp.einsum('bsgc,gcd->bsgd', pooled, w_pool)
    return y.reshape(B, S, POOL_WIDTH) * pool_scale


def _mla(cq, ckv, kr, cos, sin, q_norm_g, w_uq, kv_norm_g, w_ukv):
    B, S, _ = cq.shape
    q = jnp.einsum('bsr,rd->bsd', _rms_norm(cq, q_norm_g), w_uq).reshape(B, S, MLA_HEADS, MLA_QK_DIM)
    q_nope = q[..., :MLA_NOPE_DIM]
    q_rope = _rope(q[..., MLA_NOPE_DIM:], cos, sin)
    kv = jnp.einsum('bsr,rd->bsd', _rms_norm(ckv, kv_norm_g), w_ukv).reshape(B, S, MLA_HEADS, MLA_NOPE_DIM + MLA_V_DIM)
    k_nope = kv[..., :MLA_NOPE_DIM]
    v = kv[..., MLA_NOPE_DIM:]
    k_rope = _rope(kr[:, :, None, :], cos, sin)[:, :, 0]
    scale = MLA_QK_DIM ** -0.5
    n_blocks = S // Q_BLOCK
    qn_b = q_nope.reshape(B, n_blocks, Q_BLOCK, MLA_HEADS, MLA_NOPE_DIM).transpose(1, 0, 2, 3, 4)
    qr_b = q_rope.reshape(B, n_blocks, Q_BLOCK, MLA_HEADS, MLA_ROPE_DIM).transpose(1, 0, 2, 3, 4)
    key_idx = jnp.arange(S)

    def one_block(args):
        i, qn, qr = args
        s = (jnp.einsum('bqhd,bkhd->bhqk', qn, k_nope)
             + jnp.einsum('bqhr,bkr->bhqk', qr, k_rope)).astype(jnp.float32) * scale
        q_idx = i * Q_BLOCK + jnp.arange(Q_BLOCK)
        causal = key_idx[None, :] <= q_idx[:, None]
        s = jnp.where(causal[None, None], s, -jnp.inf)
        p = jax.nn.softmax(s, axis=-1).astype(v.dtype)
        return jnp.einsum('bhqk,bkhd->bqhd', p, v)

    out = lax.map(one_block, (jnp.arange(n_blocks), qn_b, qr_b))
    return out.transpose(1, 0, 2, 3, 4).reshape(B, S, MLA_WIDTH)


def _sgu(uv, sgu_norm_g, sgu_norm_b, w_s, b_s):
    B, S, _ = uv.shape
    uv = jax.nn.gelu(uv, approximate=False)
    u, v = uv[..., :SGU_WIDTH], uv[..., SGU_WIDTH:]
    v = _layer_norm(v, sgu_norm_g, sgu_norm_b)
    n_chunks = S // SGU_CHUNK
    v = v.reshape(B, n_chunks, SGU_CHUNK, SGU_HEADS, SGU_HEAD_DIM)
    tril = jnp.tril(jnp.ones((SGU_CHUNK, SGU_CHUNK), dtype=w_s.dtype))
    mixed = jnp.einsum('hts,bcshd->bcthd', w_s * tril, v) + b_s.T[None, None, :, :, None]
    return u * mixed.reshape(B, S, SGU_WIDTH)


def setup_inputs(seed: int = 0) -> dict:
    key = jax.random.key(seed)
    ks = jax.random.split(key, 20)
    f32 = jnp.float32
    nrm = lambda k, shape, s: jax.random.normal(k, shape, f32) * s
    x = jax.random.normal(ks[0], (BATCH, SEQ, D_MODEL), f32)
    offsets = jax.random.randint(ks[1], (BATCH, 1), 0, 1024, dtype=jnp.int32)
    positions = (offsets + jnp.arange(SEQ, dtype=jnp.int32)[None, :]).astype(jnp.int32)
    return {
        "x": x,
        "positions": positions,
        "ln_in_g": 1.0 + nrm(ks[2], (D_MODEL,), 0.02),
        "ln_in_b": nrm(ks[3], (D_MODEL,), 0.02),
        "w_in": nrm(ks[4], (DEPTH, D_MODEL, D_IN_COLS), D_MODEL ** -0.5),
        "pool_w": nrm(ks[5], (DEPTH, POOL_GROUPS, POOL_GROUP_DIM, POOL_GROUP_DIM), POOL_GROUP_DIM ** -0.5),
        "pool_scale": 1.0 + nrm(ks[6], (DEPTH, POOL_WIDTH), 0.1),
        "q_norm_g": 1.0 + nrm(ks[7], (DEPTH, MLA_Q_RANK), 0.02),
        "w_uq": nrm(ks[8], (DEPTH, MLA_Q_RANK, MLA_HEADS * MLA_QK_DIM), MLA_Q_RANK ** -0.5),
        "kv_norm_g": 1.0 + nrm(ks[9], (DEPTH, MLA_KV_RANK), 0.02),
        "w_ukv": nrm(ks[10], (DEPTH, MLA_KV_RANK, MLA_HEADS * (MLA_NOPE_DIM + MLA_V_DIM)), MLA_KV_RANK ** -0.5),
        "sgu_norm_g": 1.0 + nrm(ks[11], (DEPTH, SGU_WIDTH), 0.02),
        "sgu_norm_b": nrm(ks[12], (DEPTH, SGU_WIDTH), 0.02),
        "sgu_w": nrm(ks[13], (DEPTH, SGU_HEADS, SGU_CHUNK, SGU_CHUNK), SGU_CHUNK ** -0.5),
        "sgu_b": 1.0 + nrm(ks[14], (DEPTH, SGU_HEADS, SGU_CHUNK), 0.1),
        "w_out": nrm(ks[15], (DEPTH, D_MIX, D_MODEL), BETA * D_MIX ** -0.5),
        "b_out": nrm(ks[16], (DEPTH, D_MODEL), 0.02),
        "ln_post_g": 1.0 + nrm(ks[17], (DEPTH, D_MODEL), 0.02),
        "ln_post_b": nrm(ks[18], (DEPTH, D_MODEL), 0.02),
    }


def reference(x, positions, ln_in_g, ln_in_b, w_in, pool_w, pool_scale, q_norm_g, w_uq, kv_norm_g, w_ukv,
              sgu_norm_g, sgu_norm_b, sgu_w, sgu_b, w_out, b_out, ln_post_g, ln_post_b):
    half = MLA_ROPE_DIM // 2
    inv_freq = ROPE_THETA ** (-jnp.arange(half, dtype=jnp.float32) / half)
    ang = positions.astype(jnp.float32)[..., None] * inv_freq
    cos = jnp.cos(ang)[:, :, None, :]
    sin = jnp.sin(ang)[:, :, None, :]

    h = _layer_norm(x, ln_in_g, ln_in_b)
    for l in range(DEPTH):
        proj = jnp.einsum('bsd,de->bse', h, w_in[l])
        a_x, a_g, b_cq, b_ckv, b_kr, b_g, c_uv, c_g = jnp.split(proj, SPLIT_IDX, axis=-1)
        y_a = _pool_mixer(a_x, pool_w[l], pool_scale[l]) * jax.nn.silu(a_g)
        y_b = _mla(b_cq, b_ckv, b_kr, cos, sin, q_norm_g[l], w_uq[l], kv_norm_g[l], w_ukv[l]) * jax.nn.silu(b_g)
        y_c = _sgu(c_uv, sgu_norm_g[l], sgu_norm_b[l], sgu_w[l], sgu_b[l]) * jax.nn.silu(c_g)
        y = jnp.concatenate([y_a, y_b, y_c], axis=-1)
        y = jnp.einsum('bse,ed->bsd', y, w_out[l]) + b_out[l]
        h = _layer_norm(ALPHA * h + y, ln_post_g[l], ln_post_b[l])
    return h
```

```python
import functools
import math

import jax
import jax.numpy as jnp
from jax import lax
from jax.experimental import pallas as pl
from jax.experimental.pallas import tpu as pltpu

F32 = jnp.float32
BF16 = jnp.bfloat16

D_MODEL = 2048
DEPTH = 2
EPS = 1e-5
POOL_WIDTH = 512
POOL_WINDOWS = (2, 4, 8, 16)
POOL_GROUP_DIM = 128
MLA_HEADS = 8
MLA_NOPE_DIM = 128
MLA_ROPE_DIM = 64
MLA_V_DIM = 128
MLA_WIDTH = MLA_HEADS * MLA_V_DIM
MLA_Q_RANK = 512
MLA_KV_RANK = 256
MLA_QK_DIM = MLA_NOPE_DIM + MLA_ROPE_DIM
ROPE_THETA = 10000.0
SGU_WIDTH = 512
SGU_HEADS = 4
SGU_HEAD_DIM = 128
SGU_CHUNK = 128
ALPHA = (2.0 * DEPTH) ** 0.25

LANES = 128
MXU_DIM = 256
VMEM_LIMIT = 56 * 1024 * 1024

KR_PAD = MXU_DIM
COL_AX = 0
COL_AG = COL_AX + POOL_WIDTH
COL_CQ = COL_AG + POOL_WIDTH
COL_CKV = COL_CQ + MLA_Q_RANK
COL_KR = COL_CKV + MLA_KV_RANK
COL_BG = COL_KR + KR_PAD
COL_UV = COL_BG + MLA_WIDTH
COL_CG = COL_UV + 2 * SGU_WIDTH
PROJ_COLS = COL_CG + SGU_WIDTH
QK_PAD = MXU_DIM

TM_PROJ = 512
N_CHUNK = 768
TM_MLA = 512
TQ = 512
TK = 512
TM_SGU = 512
TM_OUT = 512


def _layer_norm(x, g, b):
    mu = jnp.mean(x, axis=-1, keepdims=True)
    xc = x - mu
    var = jnp.mean(xc * xc, axis=-1, keepdims=True)
    return xc * lax.rsqrt(var + EPS) * g + b


def _rms_norm(x, g):
    ms = jnp.mean(x * x, axis=-1, keepdims=True)
    return x * lax.rsqrt(ms + EPS) * g


def _silu(x):
    return x / (1.0 + jnp.exp(-x))


def _resident(shape):
    return pl.BlockSpec(shape, lambda *_: (0,) * len(shape), pipeline_mode=pl.Buffered(1))


def _params():
    return pltpu.CompilerParams(vmem_limit_bytes=VMEM_LIMIT)


def _rope_table_kernel(pos_ref, freq_ref, sign_ref, cos_ref, sin_ref):
    ang = pos_ref[...].astype(F32) * freq_ref[...]
    cos_ref[...] = jnp.cos(ang)
    sin_ref[...] = jnp.sin(ang) * sign_ref[...]


def _rope_tables(positions):
    t = positions.size
    half = MLA_ROPE_DIM // 2
    inv_freq = ROPE_THETA ** (-jnp.arange(half, dtype=F32) / half)
    freq_row = jnp.tile(inv_freq, LANES // half)[None, :]
    sign_row = jnp.tile(jnp.concatenate([-jnp.ones(half, F32), jnp.ones(half, F32)]),
                        LANES // MLA_ROPE_DIM)[None, :]
    tm = 2048
    row = pl.BlockSpec((tm, LANES), lambda i: (i, 0))
    return pl.pallas_call(
        _rope_table_kernel,
        grid=(t // tm,),
        in_specs=[pl.BlockSpec((tm, 1), lambda i: (i, 0)), _resident((1, LANES)), _resident((1, LANES))],
        out_specs=[row, row],
        out_shape=[jax.ShapeDtypeStruct((t, LANES), F32)] * 2,
        compiler_params=_params(),
        name="rope_tables",
    )(positions.reshape(t, 1), freq_row, sign_row)


def _in_proj_kernel(x_ref, g_ref, b_ref, w_ref, o_ref, *, apply_ln):
    x = x_ref[...]
    if apply_ln:
        x = _layer_norm(x, g_ref[...], b_ref[...])
    xb = x.astype(BF16)
    for c in range(PROJ_COLS // N_CHUNK):
        cols = slice(c * N_CHUNK, (c + 1) * N_CHUNK)
        o_ref[:, cols] = jnp.dot(xb, w_ref[:, cols], preferred_element_type=F32).astype(o_ref.dtype)


def _in_proj(x2d, ln_g, ln_b, w_pad, *, apply_ln):
    t = x2d.shape[0]
    return pl.pallas_call(
        functools.partial(_in_proj_kernel, apply_ln=apply_ln),
        grid=(t // TM_PROJ,),
        in_specs=[pl.BlockSpec((TM_PROJ, D_MODEL), lambda i: (i, 0)),
                  _resident((1, D_MODEL)), _resident((1, D_MODEL)),
                  _resident((D_MODEL, PROJ_COLS))],
        out_specs=pl.BlockSpec((TM_PROJ, PROJ_COLS), lambda i: (i, 0)),
        out_shape=jax.ShapeDtypeStruct((t, PROJ_COLS), BF16),
        compiler_params=_params(),
        name="in_proj",
    )(x2d, ln_g, ln_b, w_pad)


def _pool_kernel(ax_ref, ag_ref, w_ref, sc_ref, o_ref):
    s = ax_ref.shape[1]
    t = lax.broadcasted_iota(jnp.int32, (s, POOL_GROUP_DIM), 0)
    for gi, win in enumerate(POOL_WINDOWS):
        cols = slice(gi * POOL_GROUP_DIM, (gi + 1) * POOL_GROUP_DIM)
        x = ax_ref[0, :, cols].astype(F32)
        acc = x
        span = 1
        while span < win:
            shifted = pltpu.roll(acc, span, axis=0)
            acc = acc + jnp.where(t >= span, shifted, 0.0)
            span *= 2
        count = jnp.minimum(t + 1, win).astype(F32)
        d = (acc / count - x).astype(BF16)
        y = jnp.dot(d, w_ref[gi], preferred_element_type=F32)
        g = ag_ref[0, :, cols].astype(F32)
        o_ref[0, :, cols] = (y * sc_ref[:, cols] * _silu(g)).astype(o_ref.dtype)


def _pool_mixer(proj3d, w_pool, scale_row):
    b, s, _ = proj3d.shape
    return pl.pallas_call(
        _pool_kernel,
        grid=(b,),
        in_specs=[pl.BlockSpec((1, s, POOL_WIDTH), lambda i: (i, 0, COL_AX // POOL_WIDTH)),
                  pl.BlockSpec((1, s, POOL_WIDTH), lambda i: (i, 0, COL_AG // POOL_WIDTH)),
                  _resident(w_pool.shape), _resident((1, POOL_WIDTH))],
        out_specs=pl.BlockSpec((1, s, POOL_WIDTH), lambda i: (i, 0, 0)),
        out_shape=jax.ShapeDtypeStruct((b, s, POOL_WIDTH), BF16),
        compiler_params=_params(),
        name="pool_mixer",
    )(proj3d, proj3d, w_pool, scale_row)


def _mla_proj_kernel(cq_ref, ckv_ref, kr_ref, cos_ref, sin_ref, qg_ref, wqn_ref, wqr_ref,
                     kvg_ref, wkv_ref, q_ref, k_ref, v_ref):
    tm = cq_ref.shape[0]
    scale = MLA_QK_DIM ** -0.5
    cos = cos_ref[...]
    sin = sin_ref[...]
    lane = lax.broadcasted_iota(jnp.int32, (tm, LANES), 1)
    first_half = (lane % MLA_ROPE_DIM) < (MLA_ROPE_DIM // 2)
    low_head = lane < MLA_ROPE_DIM

    def rope(x):
        swapped = jnp.where(first_half,
                            pltpu.roll(x, LANES - MLA_ROPE_DIM // 2, axis=1),
                            pltpu.roll(x, MLA_ROPE_DIM // 2, axis=1))
        return x * cos + swapped * sin

    cqn = _rms_norm(cq_ref[...].astype(F32), qg_ref[...]).astype(BF16)
    qn = jnp.dot(cqn, wqn_ref[...], preferred_element_type=F32) * scale
    qr = jnp.dot(cqn, wqr_ref[...], preferred_element_type=F32)
    for pair in range(MLA_HEADS // 2):
        r = rope(qr[:, pair * LANES:(pair + 1) * LANES]) * scale
        for sub in range(2):
            h = 2 * pair + sub
            q_ref[0, h, :, :MLA_NOPE_DIM] = qn[:, h * MLA_NOPE_DIM:(h + 1) * MLA_NOPE_DIM].astype(q_ref.dtype)
            keep = low_head if sub == 0 else jnp.logical_not(low_head)
            q_ref[0, h, :, MLA_NOPE_DIM:] = jnp.where(keep, r, 0.0).astype(q_ref.dtype)

    ckvn = _rms_norm(ckv_ref[...].astype(F32), kvg_ref[...]).astype(BF16)
    kv = jnp.dot(ckvn, wkv_ref[...], preferred_element_type=F32)
    kr = rope(kr_ref[...].astype(F32))
    kr2 = jnp.where(low_head, kr, pltpu.roll(kr, MLA_ROPE_DIM, axis=1)).astype(k_ref.dtype)
    hw = MLA_NOPE_DIM + MLA_V_DIM
    for h in range(MLA_HEADS):
        k_ref[0, h, :, :MLA_NOPE_DIM] = kv[:, h * hw:h * hw + MLA_NOPE_DIM].astype(k_ref.dtype)
        k_ref[0, h, :, MLA_NOPE_DIM:] = kr2
        v_ref[0, h] = kv[:, h * hw + MLA_NOPE_DIM:(h + 1) * hw].astype(v_ref.dtype)


def _mla_proj(proj3d, cos, sin, q_norm_g, wq_nope, wq_rope, kv_norm_g, w_ukv):
    b, s, _ = proj3d.shape
    nb = s // TM_MLA
    tab = pl.BlockSpec((TM_MLA, LANES), lambda i, j: (i * nb + j, 0))

    def seg(width, col):
        return pl.BlockSpec((None, TM_MLA, width), lambda i, j: (i, j, col // width))

    def heads(width):
        return pl.BlockSpec((1, MLA_HEADS, TM_MLA, width), lambda i, j: (i, 0, j, 0))

    return pl.pallas_call(
        _mla_proj_kernel,
        grid=(b, nb),
        in_specs=[seg(MLA_Q_RANK, COL_CQ), seg(MLA_KV_RANK, COL_CKV), seg(LANES, COL_KR), tab, tab,
                  _resident((1, MLA_Q_RANK)), _resident(wq_nope.shape), _resident(wq_rope.shape),
                  _resident((1, MLA_KV_RANK)), _resident(w_ukv.shape)],
        out_specs=[heads(QK_PAD), heads(QK_PAD), heads(MLA_V_DIM)],
        out_shape=[jax.ShapeDtypeStruct((b, MLA_HEADS, s, QK_PAD), BF16),
                   jax.ShapeDtypeStruct((b, MLA_HEADS, s, QK_PAD), BF16),
                   jax.ShapeDtypeStruct((b, MLA_HEADS, s, MLA_V_DIM), BF16)],
        compiler_params=_params(),
        name="mla_proj",
    )(proj3d, proj3d, proj3d, cos, sin, q_norm_g, wq_nope, wq_rope, kv_norm_g, w_ukv)


def _attn_kernel(q_ref, k_ref, v_ref, o_ref):
    qi = pl.program_id(1)

    def head_body(h, _):
        q = q_ref[0, h]

        def step(j, carry, masked):
            m, l, acc = carry
            k = k_ref[0, h, pl.ds(pl.multiple_of(j * TK, TK), TK), :]
            v = v_ref[0, h, pl.ds(pl.multiple_of(j * TK, TK), TK), :]
            s = lax.dot_general(q, k, (((1,), (1,)), ((), ())), preferred_element_type=F32)
            if masked:
                row = lax.broadcasted_iota(jnp.int32, (TQ, TK), 0)
                col = lax.broadcasted_iota(jnp.int32, (TQ, TK), 1)
                s = jnp.where(col <= row, s, -jnp.inf)
            m_new = jnp.maximum(m, jnp.max(s, axis=-1, keepdims=True))
            p = jnp.exp(s - m_new)
            alpha = jnp.exp(m - m_new)
            l = alpha * l + jnp.sum(p, axis=-1, keepdims=True)
            acc = alpha * acc + jnp.dot(p.astype(BF16), v, preferred_element_type=F32)
            return m_new, l, acc

        init = (jnp.full((TQ, 1), -jnp.inf, F32), jnp.zeros((TQ, 1), F32), jnp.zeros((TQ, MLA_V_DIM), F32))
        carry = lax.fori_loop(0, qi, lambda j, c: step(j, c, False), init)
        _, l, acc = step(qi, carry, True)
        o_ref[0, h] = (acc / l).astype(o_ref.dtype)
        return 0

    lax.fori_loop(0, MLA_HEADS, head_body, 0)


def _attention(q, k, v):
    b, h, s, _ = q.shape
    assert TQ == TK
    return pl.pallas_call(
        _attn_kernel,
        grid=(b, s // TQ),
        in_specs=[pl.BlockSpec((1, h, TQ, QK_PAD), lambda i, j: (i, 0, j, 0)),
                  pl.BlockSpec((1, h, s, QK_PAD), lambda i, j: (i, 0, 0, 0)),
                  pl.BlockSpec((1, h, s, MLA_V_DIM), lambda i, j: (i, 0, 0, 0))],
        out_specs=pl.BlockSpec((1, h, TQ, MLA_V_DIM), lambda i, j: (i, 0, j, 0)),
        out_shape=jax.ShapeDtypeStruct((b, h, s, MLA_V_DIM), BF16),
        compiler_params=_params(),
        name="mla_attention",
    )(q, k, v)


def _sgu_kernel(uv_ref, cg_ref, ng_ref, nb_ref, ws_ref, bs_ref, o_ref):
    tm = uv_ref.shape[0]
    uv = uv_ref[...].astype(F32)
    uv = 0.5 * uv * (1.0 + lax.erf(uv * (2.0 ** -0.5)))
    u = uv[:, :SGU_WIDTH]
    vb = _layer_norm(uv[:, SGU_WIDTH:], ng_ref[...], nb_ref[...]).astype(BF16)
    gate = _silu(cg_ref[...].astype(F32))
    n_chunks = tm // SGU_CHUNK
    row = lax.broadcasted_iota(jnp.int32, (SGU_CHUNK, SGU_CHUNK), 0)
    col = lax.broadcasted_iota(jnp.int32, (SGU_CHUNK, SGU_CHUNK), 1)
    for h in range(SGU_HEADS):
        cols = slice(h * SGU_HEAD_DIM, (h + 1) * SGU_HEAD_DIM)
        w = jnp.where(col <= row, ws_ref[h], 0.0).astype(BF16)
        vh = jnp.concatenate([vb[c * SGU_CHUNK:(c + 1) * SGU_CHUNK, cols] for c in range(n_chunks)], axis=1)
        mixed = jnp.dot(w, vh, preferred_element_type=F32) + bs_ref[h]
        for c in range(n_chunks):
            rows = slice(c * SGU_CHUNK, (c + 1) * SGU_CHUNK)
            piece = mixed[:, c * SGU_HEAD_DIM:(c + 1) * SGU_HEAD_DIM]
            o_ref[rows, cols] = (u[rows, cols] * piece * gate[rows, cols]).astype(o_ref.dtype)


def _sgu(proj2d, norm_g, norm_b, w_s, b_s_col):
    t = proj2d.shape[0]
    return pl.pallas_call(
        _sgu_kernel,
        grid=(t // TM_SGU,),
        in_specs=[pl.BlockSpec((TM_SGU, 2 * SGU_WIDTH), lambda i: (i, COL_UV // (2 * SGU_WIDTH))),
                  pl.BlockSpec((TM_SGU, SGU_WIDTH), lambda i: (i, COL_CG // SGU_WIDTH)),
                  _resident((1, SGU_WIDTH)), _resident((1, SGU_WIDTH)),
                  _resident(w_s.shape), _resident(b_s_col.shape)],
        out_specs=pl.BlockSpec((TM_SGU, SGU_WIDTH), lambda i: (i, 0)),
        out_shape=jax.ShapeDtypeStruct((t, SGU_WIDTH), BF16),
        compiler_params=_params(),
        name="sgu_mixer",
    )(proj2d, proj2d, norm_g, norm_b, w_s, b_s_col)


def _out_proj_kernel(ya_ref, yb_ref, bg_ref, yc_ref, h_ref, ing_ref, inb_ref, w_ref, bo_ref,
                     pg_ref, pb_ref, o_ref, *, residual_ln):
    attn = jnp.concatenate([yb_ref[0, h] for h in range(MLA_HEADS)], axis=1).astype(F32)
    yb = (attn * _silu(bg_ref[...].astype(F32))).astype(BF16)
    y = jnp.concatenate([ya_ref[...], yb, yc_ref[...]], axis=1)
    out = jnp.dot(y, w_ref[...], preferred_element_type=F32) + bo_ref[...]
    h = h_ref[...]
    if residual_ln:
        h = _layer_norm(h, ing_ref[...], inb_ref[...])
    o_ref[...] = _layer_norm(ALPHA * h + out, pg_ref[...], pb_ref[...])


def _out_proj(y_a, y_b, proj3d, y_c, h3d, ln_in_g, ln_in_b, w_out, b_out, post_g, post_b, *, residual_ln):
    b, s, _ = h3d.shape

    def rows(width, col=0):
        return pl.BlockSpec((None, TM_OUT, width), lambda i, j: (i, j, col // width))

    vec = _resident((1, D_MODEL))
    return pl.pallas_call(
        functools.partial(_out_proj_kernel, residual_ln=residual_ln),
        grid=(b, s // TM_OUT),
        in_specs=[rows(POOL_WIDTH),
                  pl.BlockSpec((1, MLA_HEADS, TM_OUT, MLA_V_DIM), lambda i, j: (i, 0, j, 0)),
                  rows(MLA_WIDTH, COL_BG), rows(SGU_WIDTH), rows(D_MODEL),
                  vec, vec, _resident(w_out.shape), vec, vec, vec],
        out_specs=rows(D_MODEL),
        out_shape=jax.ShapeDtypeStruct((b, s, D_MODEL), F32),
        compiler_params=_params(),
        name="out_proj",
    )(y_a, y_b, proj3d, y_c, h3d, ln_in_g, ln_in_b, w_out, b_out, post_g, post_b)


def kernel(x, positions, ln_in_g, ln_in_b, w_in, pool_w, pool_scale, q_norm_g, w_uq, kv_norm_g, w_ukv,
           sgu_norm_g, sgu_norm_b, sgu_w, sgu_b, w_out, b_out, ln_post_g, ln_post_b):
    b, s, d = x.shape
    t = b * s
    assert d == D_MODEL and s % TQ == 0 and s % TM_OUT == 0 and t % TM_PROJ == 0

    cos, sin = _rope_tables(positions)
    ln_in_g2, ln_in_b2 = ln_in_g[None, :], ln_in_b[None, :]

    h = x
    for l in range(DEPTH):
        w_l = w_in[l]
        kr_end = COL_KR + MLA_ROPE_DIM
        w_pad = jnp.concatenate(
            [w_l[:, :kr_end], jnp.zeros((D_MODEL, KR_PAD - MLA_ROPE_DIM), w_l.dtype), w_l[:, kr_end:]],
            axis=1).astype(BF16)
        wq = w_uq[l].reshape(MLA_Q_RANK, MLA_HEADS, MLA_QK_DIM)
        wq_nope = wq[:, :, :MLA_NOPE_DIM].reshape(MLA_Q_RANK, MLA_HEADS * MLA_NOPE_DIM).astype(BF16)
        wq_rope = wq[:, :, MLA_NOPE_DIM:].reshape(MLA_Q_RANK, MLA_HEADS * MLA_ROPE_DIM).astype(BF16)

        first = l == 0
        proj = _in_proj(h.reshape(t, d), ln_in_g2, ln_in_b2, w_pad, apply_ln=first)
        proj3d = proj.reshape(b, s, PROJ_COLS)

        y_a = _pool_mixer(proj3d, pool_w[l].astype(BF16), pool_scale[l][None, :])
        q, k, v = _mla_proj(proj3d, cos, sin, q_norm_g[l][None, :], wq_nope, wq_rope,
                            kv_norm_g[l][None, :], w_ukv[l].astype(BF16))
        y_b = _attention(q, k, v)
        y_c = _sgu(proj, sgu_norm_g[l][None, :], sgu_norm_b[l][None, :], sgu_w[l], sgu_b[l][:, :, None])
        h = _out_proj(y_a, y_b, proj3d, y_c.reshape(b, s, SGU_WIDTH), h, ln_in_g2, ln_in_b2,
                      w_out[l].astype(BF16), b_out[l][None, :], ln_post_g[l][None, :], ln_post_b[l][None, :],
                      residual_ln=first)
    return h
```

```python
import functools
import math

import jax
import jax.numpy as jnp
from jax import lax
from jax.experimental import pallas as pl
from jax.experimental.pallas import tpu as pltpu

F32 = jnp.float32
BF16 = jnp.bfloat16

D_MODEL = 2048
DEPTH = 2
EPS = 1e-5
POOL_WIDTH = 512
POOL_WINDOWS = (2, 4, 8, 16)
POOL_GROUP_DIM = 128
MLA_HEADS = 8
MLA_NOPE_DIM = 128
MLA_ROPE_DIM = 64
MLA_V_DIM = 128
MLA_WIDTH = MLA_HEADS * MLA_V_DIM
MLA_Q_RANK = 512
MLA_KV_RANK = 256
MLA_QK_DIM = MLA_NOPE_DIM + MLA_ROPE_DIM
ROPE_THETA = 10000.0
SGU_WIDTH = 512
SGU_HEADS = 4
SGU_HEAD_DIM = 128
SGU_CHUNK = 128
ALPHA = (2.0 * DEPTH) ** 0.25

LANES = 128
MXU_DIM = 256
VMEM_LIMIT = 56 * 1024 * 1024

KR_PAD = MXU_DIM
COL_AX = 0
COL_AG = COL_AX + POOL_WIDTH
COL_CQ = COL_AG + POOL_WIDTH
COL_CKV = COL_CQ + MLA_Q_RANK
COL_KR = COL_CKV + MLA_KV_RANK
COL_BG = COL_KR + KR_PAD
COL_UV = COL_BG + MLA_WIDTH
COL_CG = COL_UV + 2 * SGU_WIDTH
PROJ_COLS = COL_CG + SGU_WIDTH
QK_PAD = MXU_DIM

TM_PROJ = 512
N_CHUNK = 768
TM_MLA = 512
TQ = 512
TK = 512
TM_SGU = 512
TM_OUT = 512
SUB_OUT = 256


def _layer_norm(x, g, b):
    mu = jnp.mean(x, axis=-1, keepdims=True)
    xc = x - mu
    var = jnp.mean(xc * xc, axis=-1, keepdims=True)
    return xc * lax.rsqrt(var + EPS) * g + b


def _rms_norm(x, g):
    ms = jnp.mean(x * x, axis=-1, keepdims=True)
    return x * lax.rsqrt(ms + EPS) * g


def _silu(x):
    return x / (1.0 + jnp.exp(-x))


def _resident(shape):
    return pl.BlockSpec(shape, lambda *_: (0,) * len(shape), pipeline_mode=pl.Buffered(1))


def _params():
    return pltpu.CompilerParams(vmem_limit_bytes=VMEM_LIMIT)


def _rope_table_kernel(pos_ref, freq_ref, sign_ref, cos_ref, sin_ref):
    ang = pos_ref[...].astype(F32) * freq_ref[...]
    cos_ref[...] = jnp.cos(ang)
    sin_ref[...] = jnp.sin(ang) * sign_ref[...]


def _rope_tables(positions):
    t = positions.size
    half = MLA_ROPE_DIM // 2
    inv_freq = ROPE_THETA ** (-jnp.arange(half, dtype=F32) / half)
    freq_row = jnp.tile(inv_freq, LANES // half)[None, :]
    sign_row = jnp.tile(jnp.concatenate([-jnp.ones(half, F32), jnp.ones(half, F32)]),
                        LANES // MLA_ROPE_DIM)[None, :]
    tm = 2048
    row = pl.BlockSpec((tm, LANES), lambda i: (i, 0))
    return pl.pallas_call(
        _rope_table_kernel,
        grid=(t // tm,),
        in_specs=[pl.BlockSpec((tm, 1), lambda i: (i, 0)), _resident((1, LANES)), _resident((1, LANES))],
        out_specs=[row, row],
        out_shape=[jax.ShapeDtypeStruct((t, LANES), F32)] * 2,
        compiler_params=_params(),
        name="rope_tables",
    )(positions.reshape(t, 1), freq_row, sign_row)


def _in_proj_kernel(x_ref, g_ref, b_ref, w_ref, o_ref, *, apply_ln):
    x = x_ref[...]
    if apply_ln:
        x = _layer_norm(x, g_ref[...], b_ref[...])
    xb = x.astype(BF16)
    for c in range(PROJ_COLS // N_CHUNK):
        cols = slice(c * N_CHUNK, (c + 1) * N_CHUNK)
        o_ref[:, cols] = jnp.dot(xb, w_ref[:, cols], preferred_element_type=F32).astype(o_ref.dtype)


def _in_proj(x2d, ln_g, ln_b, w_pad, *, apply_ln):
    t = x2d.shape[0]
    return pl.pallas_call(
        functools.partial(_in_proj_kernel, apply_ln=apply_ln),
        grid=(t // TM_PROJ,),
        in_specs=[pl.BlockSpec((TM_PROJ, D_MODEL), lambda i: (i, 0)),
                  _resident((1, D_MODEL)), _resident((1, D_MODEL)),
                  _resident((D_MODEL, PROJ_COLS))],
        out_specs=pl.BlockSpec((TM_PROJ, PROJ_COLS), lambda i: (i, 0)),
        out_shape=jax.ShapeDtypeStruct((t, PROJ_COLS), BF16),
        compiler_params=_params(),
        name="in_proj",
    )(x2d, ln_g, ln_b, w_pad)


def _pool_kernel(ax_ref, ag_ref, w_ref, sc_ref, o_ref):
    s = ax_ref.shape[1]
    t = lax.broadcasted_iota(jnp.int32, (s, POOL_GROUP_DIM), 0)
    for gi, win in enumerate(POOL_WINDOWS):
        cols = slice(gi * POOL_GROUP_DIM, (gi + 1) * POOL_GROUP_DIM)
        x = ax_ref[0, :, cols].astype(F32)
        acc = x
        span = 1
        while span < win:
            shifted = pltpu.roll(acc, span, axis=0)
            acc = acc + jnp.where(t >= span, shifted, 0.0)
            span *= 2
        count = jnp.minimum(t + 1, win).astype(F32)
        d = (acc / count - x).astype(BF16)
        y = jnp.dot(d, w_ref[gi], preferred_element_type=F32)
        g = ag_ref[0, :, cols].astype(F32)
        o_ref[0, :, cols] = (y * sc_ref[:, cols] * _silu(g)).astype(o_ref.dtype)


def _pool_mixer(proj3d, w_pool, scale_row):
    b, s, _ = proj3d.shape
    return pl.pallas_call(
        _pool_kernel,
        grid=(b,),
        in_specs=[pl.BlockSpec((1, s, POOL_WIDTH), lambda i: (i, 0, COL_AX // POOL_WIDTH)),
                  pl.BlockSpec((1, s, POOL_WIDTH), lambda i: (i, 0, COL_AG // POOL_WIDTH)),
                  _resident(w_pool.shape), _resident((1, POOL_WIDTH))],
        out_specs=pl.BlockSpec((1, s, POOL_WIDTH), lambda i: (i, 0, 0)),
        out_shape=jax.ShapeDtypeStruct((b, s, POOL_WIDTH), BF16),
        compiler_params=_params(),
        name="pool_mixer",
    )(proj3d, proj3d, w_pool, scale_row)


def _mla_proj_kernel(cq_ref, ckv_ref, kr_ref, cos_ref, sin_ref, qg_ref, wqn_ref, wqr_ref,
                     kvg_ref, wk_ref, wvt_ref, q_ref, k_ref, vt_ref):
    tm = cq_ref.shape[0]
    scale = MLA_QK_DIM ** -0.5 * math.log2(math.e)
    cos = cos_ref[...]
    sin = sin_ref[...]
    lane = lax.broadcasted_iota(jnp.int32, (tm, LANES), 1)
    first_half = (lane % MLA_ROPE_DIM) < (MLA_ROPE_DIM // 2)
    low_head = lane < MLA_ROPE_DIM

    def rope(x):
        swapped = jnp.where(first_half,
                            pltpu.roll(x, LANES - MLA_ROPE_DIM // 2, axis=1),
                            pltpu.roll(x, MLA_ROPE_DIM // 2, axis=1))
        return x * cos + swapped * sin

    cqn = _rms_norm(cq_ref[...].astype(F32), qg_ref[...]).astype(BF16)
    qn = jnp.dot(cqn, wqn_ref[...], preferred_element_type=F32) * scale
    qr = jnp.dot(cqn, wqr_ref[...], preferred_element_type=F32)
    for pair in range(MLA_HEADS // 2):
        r = rope(qr[:, pair * LANES:(pair + 1) * LANES]) * scale
        for sub in range(2):
            h = 2 * pair + sub
            q_ref[0, h, :, :MLA_NOPE_DIM] = qn[:, h * MLA_NOPE_DIM:(h + 1) * MLA_NOPE_DIM].astype(q_ref.dtype)
            keep = low_head if sub == 0 else jnp.logical_not(low_head)
            q_ref[0, h, :, MLA_NOPE_DIM:] = jnp.where(keep, r, 0.0).astype(q_ref.dtype)

    ckvn = _rms_norm(ckv_ref[...].astype(F32), kvg_ref[...]).astype(BF16)
    kn = jnp.dot(ckvn, wk_ref[...], preferred_element_type=F32)
    vt = lax.dot_general(wvt_ref[...], ckvn, (((1,), (1,)), ((), ())), preferred_element_type=F32)
    kr = rope(kr_ref[...].astype(F32))
    kr2 = jnp.where(low_head, kr, pltpu.roll(kr, MLA_ROPE_DIM, axis=1)).astype(k_ref.dtype)
    for h in range(MLA_HEADS):
        k_ref[0, h, :, :MLA_NOPE_DIM] = kn[:, h * MLA_NOPE_DIM:(h + 1) * MLA_NOPE_DIM].astype(k_ref.dtype)
        k_ref[0, h, :, MLA_NOPE_DIM:] = kr2
        vt_ref[0, h, 0] = vt[h * MLA_V_DIM:(h + 1) * MLA_V_DIM, :].astype(vt_ref.dtype)


def _mla_proj(proj3d, cos, sin, q_norm_g, wq_nope, wq_rope, kv_norm_g, wk_nope, wv_t):
    b, s, _ = proj3d.shape
    assert TM_MLA == TK
    nb = s // TM_MLA
    tab = pl.BlockSpec((TM_MLA, LANES), lambda i, j: (i * nb + j, 0))

    def seg(width, col):
        return pl.BlockSpec((None, TM_MLA, width), lambda i, j: (i, j, col // width))

    def heads(width):
        return pl.BlockSpec((1, MLA_HEADS, TM_MLA, width), lambda i, j: (i, 0, j, 0))

    return pl.pallas_call(
        _mla_proj_kernel,
        grid=(b, nb),
        in_specs=[seg(MLA_Q_RANK, COL_CQ), seg(MLA_KV_RANK, COL_CKV), seg(LANES, COL_KR), tab, tab,
                  _resident((1, MLA_Q_RANK)), _resident(wq_nope.shape), _resident(wq_rope.shape),
                  _resident((1, MLA_KV_RANK)), _resident(wk_nope.shape), _resident(wv_t.shape)],
        out_specs=[heads(QK_PAD), heads(QK_PAD),
                   pl.BlockSpec((1, MLA_HEADS, 1, MLA_V_DIM, TK), lambda i, j: (i, 0, j, 0, 0))],
        out_shape=[jax.ShapeDtypeStruct((b, MLA_HEADS, s, QK_PAD), BF16),
                   jax.ShapeDtypeStruct((b, MLA_HEADS, s, QK_PAD), BF16),
                   jax.ShapeDtypeStruct((b, MLA_HEADS, nb, MLA_V_DIM, TK), BF16)],
        compiler_params=_params(),
        name="mla_proj",
    )(proj3d, proj3d, proj3d, cos, sin, q_norm_g, wq_nope, wq_rope, kv_norm_g, wk_nope, wv_t)


def _attn_kernel(q_ref, k_ref, vt_ref, o_ref, sa_ref):
    qi = pl.program_id(1)

    def scores(h, j):
        k = k_ref[0, h, pl.ds(pl.multiple_of(j * TK, TK), TK), :]
        return lax.dot_general(k, q_ref[0, h], (((1,), (1,)), ((), ())), preferred_element_type=F32)

    def update(s, h, j, carry, masked):
        m, l, acc = carry
        if masked:
            key = lax.broadcasted_iota(jnp.int32, (TK, TQ), 0)
            qry = lax.broadcasted_iota(jnp.int32, (TK, TQ), 1)
            s = jnp.where(key <= qry, s, -jnp.inf)
        m_new = jnp.maximum(m, jnp.max(s, axis=0, keepdims=True))
        p = jnp.exp2(s - m_new)
        alpha = jnp.exp2(m - m_new)
        l = alpha * l + jnp.sum(p, axis=0, keepdims=True)
        acc = alpha * acc + jnp.dot(vt_ref[0, h, j], p.astype(BF16), preferred_element_type=F32)
        return m_new, l, acc

    def finish(h, carry):
        _, l, acc = carry
        o_ref[0, h] = (acc / l).T.astype(o_ref.dtype)

    def pair_body(hp, _):
        ha, hb = 2 * hp, 2 * hp + 1
        sa_ref[...] = scores(ha, 0)

        def body(j, carry):
            ca, cb = carry
            sb = scores(hb, j)
            ca = update(sa_ref[...], ha, j, ca, False)
            sa_next = scores(ha, j + 1)
            cb = update(sb, hb, j, cb, False)
            sa_ref[...] = sa_next
            return ca, cb

        init = (jnp.full((1, TQ), -jnp.inf, F32), jnp.zeros((1, TQ), F32), jnp.zeros((MLA_V_DIM, TQ), F32))
        ca, cb = lax.fori_loop(0, qi, body, (init, init))
        sb = scores(hb, qi)
        finish(ha, update(sa_ref[...], ha, qi, ca, True))
        finish(hb, update(sb, hb, qi, cb, True))
        return 0

    lax.fori_loop(0, MLA_HEADS // 2, pair_body, 0)


def _attention(q, k, vt):
    b, h, s, _ = q.shape
    assert TQ == TK
    return pl.pallas_call(
        _attn_kernel,
        grid=(b, s // TQ),
        in_specs=[pl.BlockSpec((1, h, TQ, QK_PAD), lambda i, j: (i, 0, j, 0)),
                  pl.BlockSpec((1, h, s, QK_PAD), lambda i, j: (i, 0, 0, 0)),
                  pl.BlockSpec((1, h, s // TK, MLA_V_DIM, TK), lambda i, j: (i, 0, 0, 0, 0))],
        out_specs=pl.BlockSpec((1, h, TQ, MLA_V_DIM), lambda i, j: (i, 0, j, 0)),
        out_shape=jax.ShapeDtypeStruct((b, h, s, MLA_V_DIM), BF16),
        scratch_shapes=[pltpu.VMEM((TK, TQ), F32)],
        compiler_params=_params(),
        name="mla_attention",
    )(q, k, vt)


def _sgu_kernel(uv_ref, cg_ref, ng_ref, nb_ref, ws_ref, bs_ref, o_ref):
    tm = uv_ref.shape[0]
    uv = uv_ref[...].astype(F32)
    uv = 0.5 * uv * (1.0 + lax.erf(uv * (2.0 ** -0.5)))
    u = uv[:, :SGU_WIDTH]
    vb = _layer_norm(uv[:, SGU_WIDTH:], ng_ref[...], nb_ref[...]).astype(BF16)
    gate = _silu(cg_ref[...].astype(F32))
    n_chunks = tm // SGU_CHUNK
    row = lax.broadcasted_iota(jnp.int32, (SGU_CHUNK, SGU_CHUNK), 0)
    col = lax.broadcasted_iota(jnp.int32, (SGU_CHUNK, SGU_CHUNK), 1)
    for h in range(SGU_HEADS):
        cols = slice(h * SGU_HEAD_DIM, (h + 1) * SGU_HEAD_DIM)
        w = jnp.where(col <= row, ws_ref[h], 0.0).astype(BF16)
        vh = jnp.concatenate([vb[c * SGU_CHUNK:(c + 1) * SGU_CHUNK, cols] for c in range(n_chunks)], axis=1)
        mixed = jnp.dot(w, vh, preferred_element_type=F32) + bs_ref[h]
        for c in range(n_chunks):
            rows = slice(c * SGU_CHUNK, (c + 1) * SGU_CHUNK)
            piece = mixed[:, c * SGU_HEAD_DIM:(c + 1) * SGU_HEAD_DIM]
            o_ref[rows, cols] = (u[rows, cols] * piece * gate[rows, cols]).astype(o_ref.dtype)


def _sgu(proj2d, norm_g, norm_b, w_s, b_s_col):
    t = proj2d.shape[0]
    return pl.pallas_call(
        _sgu_kernel,
        grid=(t // TM_SGU,),
        in_specs=[pl.BlockSpec((TM_SGU, 2 * SGU_WIDTH), lambda i: (i, COL_UV // (2 * SGU_WIDTH))),
                  pl.BlockSpec((TM_SGU, SGU_WIDTH), lambda i: (i, COL_CG // SGU_WIDTH)),
                  _resident((1, SGU_WIDTH)), _resident((1, SGU_WIDTH)),
                  _resident(w_s.shape), _resident(b_s_col.shape)],
        out_specs=pl.BlockSpec((TM_SGU, SGU_WIDTH), lambda i: (i, 0)),
        out_shape=jax.ShapeDtypeStruct((t, SGU_WIDTH), BF16),
        compiler_params=_params(),
        name="sgu_mixer",
    )(proj2d, proj2d, norm_g, norm_b, w_s, b_s_col)


def _out_proj_kernel(ya_ref, yb_ref, bg_ref, yc_ref, h_ref, ing_ref, inb_ref, w_ref, bo_ref,
                     pg_ref, pb_ref, o_ref, *, residual_ln):
    for r in range(TM_OUT // SUB_OUT):
        rows = slice(r * SUB_OUT, (r + 1) * SUB_OUT)
        attn = jnp.concatenate([yb_ref[0, h, rows, :] for h in range(MLA_HEADS)], axis=1).astype(F32)
        yb = (attn * _silu(bg_ref[rows, :].astype(F32))).astype(BF16)
        y = jnp.concatenate([ya_ref[rows, :], yb, yc_ref[rows, :]], axis=1)
        out = jnp.dot(y, w_ref[...], preferred_element_type=F32) + bo_ref[...]
        h = h_ref[rows, :]
        if residual_ln:
            h = _layer_norm(h, ing_ref[...], inb_ref[...])
        o_ref[rows, :] = _layer_norm(ALPHA * h + out, pg_ref[...], pb_ref[...])


def _out_proj(y_a, y_b, proj3d, y_c, h3d, ln_in_g, ln_in_b, w_out, b_out, post_g, post_b, *, residual_ln):
    b, s, _ = h3d.shape

    def rows(width, col=0):
        return pl.BlockSpec((None, TM_OUT, width), lambda i, j: (i, j, col // width))

    vec = _resident((1, D_MODEL))
    return pl.pallas_call(
        functools.partial(_out_proj_kernel, residual_ln=residual_ln),
        grid=(b, s // TM_OUT),
        in_specs=[rows(POOL_WIDTH),
                  pl.BlockSpec((1, MLA_HEADS, TM_OUT, MLA_V_DIM), lambda i, j: (i, 0, j, 0)),
                  rows(MLA_WIDTH, COL_BG), rows(SGU_WIDTH), rows(D_MODEL),
                  vec, vec, _resident(w_out.shape), vec, vec, vec],
        out_specs=rows(D_MODEL),
        out_shape=jax.ShapeDtypeStruct((b, s, D_MODEL), F32),
        compiler_params=_params(),
        name="out_proj",
    )(y_a, y_b, proj3d, y_c, h3d, ln_in_g, ln_in_b, w_out, b_out, post_g, post_b)


def kernel(x, positions, ln_in_g, ln_in_b, w_in, pool_w, pool_scale, q_norm_g, w_uq, kv_norm_g, w_ukv,
           sgu_norm_g, sgu_norm_b, sgu_w, sgu_b, w_out, b_out, ln_post_g, ln_post_b):
    b, s, d = x.shape
    t = b * s
    assert d == D_MODEL and s % TQ == 0 and s % TM_OUT == 0 and t % TM_PROJ == 0

    cos, sin = _rope_tables(positions)
    ln_in_g2, ln_in_b2 = ln_in_g[None, :], ln_in_b[None, :]

    h = x
    for l in range(DEPTH):
        w_l = w_in[l]
        kr_end = COL_KR + MLA_ROPE_DIM
        w_pad = jnp.concatenate(
            [w_l[:, :kr_end], jnp.zeros((D_MODEL, KR_PAD - MLA_ROPE_DIM), w_l.dtype), w_l[:, kr_end:]],
            axis=1).astype(BF16)
        wq = w_uq[l].reshape(MLA_Q_RANK, MLA_HEADS, MLA_QK_DIM)
        wq_nope = wq[:, :, :MLA_NOPE_DIM].reshape(MLA_Q_RANK, MLA_HEADS * MLA_NOPE_DIM).astype(BF16)
        wq_rope = wq[:, :, MLA_NOPE_DIM:].reshape(MLA_Q_RANK, MLA_HEADS * MLA_ROPE_DIM).astype(BF16)

        first = l == 0
        proj = _in_proj(h.reshape(t, d), ln_in_g2, ln_in_b2, w_pad, apply_ln=first)
        proj3d = proj.reshape(b, s, PROJ_COLS)

        y_a = _pool_mixer(proj3d, pool_w[l].astype(BF16), pool_scale[l][None, :])
        wkv = w_ukv[l].reshape(MLA_KV_RANK, MLA_HEADS, MLA_NOPE_DIM + MLA_V_DIM)
        wk_nope = wkv[:, :, :MLA_NOPE_DIM].reshape(MLA_KV_RANK, MLA_HEADS * MLA_NOPE_DIM).astype(BF16)
        wv_t = wkv[:, :, MLA_NOPE_DIM:].reshape(MLA_KV_RANK, MLA_HEADS * MLA_V_DIM).T.astype(BF16)
        q, k, vt = _mla_proj(proj3d, cos, sin, q_norm_g[l][None, :], wq_nope, wq_rope,
                             kv_norm_g[l][None, :], wk_nope, wv_t)
        y_b = _attention(q, k, vt)
        y_c = _sgu(proj, sgu_norm_g[l][None, :], sgu_norm_b[l][None, :], sgu_w[l], sgu_b[l][:, :, None])
        h = _out_proj(y_a, y_b, proj3d, y_c.reshape(b, s, SGU_WIDTH), h, ln_in_g2, ln_in_b2,
                      w_out[l].astype(BF16), b_out[l][None, :], ln_post_g[l][None, :], ln_post_b[l][None, :],
                      residual_ln=first)
    return h
```

```python
import functools
import math

import jax
import jax.numpy as jnp
from jax import lax
from jax.experimental import pallas as pl
from jax.experimental.pallas import tpu as pltpu

F32 = jnp.float32
BF16 = jnp.bfloat16

D_MODEL = 2048
DEPTH = 2
EPS = 1e-5
POOL_WIDTH = 512
POOL_WINDOWS = (2, 4, 8, 16)
POOL_GROUP_DIM = 128
MLA_HEADS = 8
MLA_NOPE_DIM = 128
MLA_ROPE_DIM = 64
MLA_V_DIM = 128
MLA_WIDTH = MLA_HEADS * MLA_V_DIM
MLA_Q_RANK = 512
MLA_KV_RANK = 256
MLA_QK_DIM = MLA_NOPE_DIM + MLA_ROPE_DIM
ROPE_THETA = 10000.0
SGU_WIDTH = 512
SGU_HEADS = 4
SGU_HEAD_DIM = 128
SGU_CHUNK = 128
ALPHA = (2.0 * DEPTH) ** 0.25

LANES = 128
MXU_DIM = 256
VMEM_LIMIT = 56 * 1024 * 1024

KR_PAD = MXU_DIM
COL_AX = 0
COL_AG = COL_AX + POOL_WIDTH
COL_CQ = COL_AG + POOL_WIDTH
COL_CKV = COL_CQ + MLA_Q_RANK
COL_KR = COL_CKV + MLA_KV_RANK
COL_BG = COL_KR + KR_PAD
COL_UV = COL_BG + MLA_WIDTH
COL_CG = COL_UV + 2 * SGU_WIDTH
PROJ_COLS = COL_CG + SGU_WIDTH
QK_PAD = MXU_DIM

TM_PROJ = 512
N_CHUNK = 768
TM_MLA = 512
TQ = 512
TK = 512
TM_SGU = 512
TM_OUT = 512


def _layer_norm(x, g, b):
    mu = jnp.mean(x, axis=-1, keepdims=True)
    xc = x - mu
    var = jnp.mean(xc * xc, axis=-1, keepdims=True)
    return xc * lax.rsqrt(var + EPS) * g + b


def _rms_norm(x, g):
    ms = jnp.mean(x * x, axis=-1, keepdims=True)
    return x * lax.rsqrt(ms + EPS) * g


def _silu(x):
    return x / (1.0 + jnp.exp(-x))


def _resident(shape):
    return pl.BlockSpec(shape, lambda *_: (0,) * len(shape), pipeline_mode=pl.Buffered(1))


def _params():
    return pltpu.CompilerParams(vmem_limit_bytes=VMEM_LIMIT)


def _rope_table_kernel(pos_ref, freq_ref, sign_ref, cos_ref, sin_ref):
    ang = pos_ref[...].astype(F32) * freq_ref[...]
    cos_ref[...] = jnp.cos(ang)
    sin_ref[...] = jnp.sin(ang) * sign_ref[...]


def _rope_tables(positions):
    t = positions.size
    half = MLA_ROPE_DIM // 2
    inv_freq = ROPE_THETA ** (-jnp.arange(half, dtype=F32) / half)
    freq_row = jnp.tile(inv_freq, LANES // half)[None, :]
    sign_row = jnp.tile(jnp.concatenate([-jnp.ones(half, F32), jnp.ones(half, F32)]),
                        LANES // MLA_ROPE_DIM)[None, :]
    tm = 2048
    row = pl.BlockSpec((tm, LANES), lambda i: (i, 0))
    return pl.pallas_call(
        _rope_table_kernel,
        grid=(t // tm,),
        in_specs=[pl.BlockSpec((tm, 1), lambda i: (i, 0)), _resident((1, LANES)), _resident((1, LANES))],
        out_specs=[row, row],
        out_shape=[jax.ShapeDtypeStruct((t, LANES), F32)] * 2,
        compiler_params=_params(),
        name="rope_tables",
    )(positions.reshape(t, 1), freq_row, sign_row)


def _in_proj_kernel(x_ref, g_ref, b_ref, wa_ref, wkr_ref, wb_ref, o_ref, *, apply_ln):
    x = x_ref[...]
    if apply_ln:
        x = _layer_norm(x, g_ref[...], b_ref[...])
    xb = x.astype(BF16)

    def emit(w_ref, out_col):
        width = w_ref.shape[1]
        for c0 in range(0, width, N_CHUNK):
            n = min(N_CHUNK, width - c0)
            o_ref[:, out_col + c0:out_col + c0 + n] = jnp.dot(
                xb, w_ref[:, c0:c0 + n], preferred_element_type=F32).astype(o_ref.dtype)

    emit(wa_ref, COL_AX)
    emit(wkr_ref, COL_KR)
    emit(wb_ref, COL_BG)


def _in_proj(x2d, ln_g, ln_b, w_a, w_kr, w_b, *, apply_ln):
    t = x2d.shape[0]
    assert w_a.shape[1] == COL_KR and w_kr.shape[1] == KR_PAD and w_b.shape[1] == PROJ_COLS - COL_BG
    return pl.pallas_call(
        functools.partial(_in_proj_kernel, apply_ln=apply_ln),
        grid=(t // TM_PROJ,),
        in_specs=[pl.BlockSpec((TM_PROJ, D_MODEL), lambda i: (i, 0)),
                  _resident((1, D_MODEL)), _resident((1, D_MODEL)),
                  _resident(w_a.shape), _resident(w_kr.shape), _resident(w_b.shape)],
        out_specs=pl.BlockSpec((TM_PROJ, PROJ_COLS), lambda i: (i, 0)),
        out_shape=jax.ShapeDtypeStruct((t, PROJ_COLS), BF16),
        compiler_params=_params(),
        name="in_proj",
    )(x2d, ln_g, ln_b, w_a, w_kr, w_b)


def _pool_kernel(ax_ref, ag_ref, w_ref, sc_ref, o_ref):
    s = ax_ref.shape[1]
    t = lax.broadcasted_iota(jnp.int32, (s, POOL_GROUP_DIM), 0)
    for gi, win in enumerate(POOL_WINDOWS):
        cols = slice(gi * POOL_GROUP_DIM, (gi + 1) * POOL_GROUP_DIM)
        x = ax_ref[0, :, cols].astype(F32)
        acc = x
        span = 1
        while span < win:
            shifted = pltpu.roll(acc, span, axis=0)
            acc = acc + jnp.where(t >= span, shifted, 0.0)
            span *= 2
        count = jnp.minimum(t + 1, win).astype(F32)
        d = (acc / count - x).astype(BF16)
        y = jnp.dot(d, w_ref[gi], preferred_element_type=F32)
        g = ag_ref[0, :, cols].astype(F32)
        o_ref[0, :, cols] = (y * sc_ref[:, cols] * _silu(g)).astype(o_ref.dtype)


def _pool_mixer(proj3d, w_pool, scale_row):
    b, s, _ = proj3d.shape
    return pl.pallas_call(
        _pool_kernel,
        grid=(b,),
        in_specs=[pl.BlockSpec((1, s, POOL_WIDTH), lambda i: (i, 0, COL_AX // POOL_WIDTH)),
                  pl.BlockSpec((1, s, POOL_WIDTH), lambda i: (i, 0, COL_AG // POOL_WIDTH)),
                  _resident(w_pool.shape), _resident((1, POOL_WIDTH))],
        out_specs=pl.BlockSpec((1, s, POOL_WIDTH), lambda i: (i, 0, 0)),
        out_shape=jax.ShapeDtypeStruct((b, s, POOL_WIDTH), BF16),
        compiler_params=_params(),
        name="pool_mixer",
    )(proj3d, proj3d, w_pool, scale_row)


def _mla_proj_kernel(cq_ref, ckv_ref, kr_ref, cos_ref, sin_ref, qg_ref, wqn_ref, wqr_ref,
                     kvg_ref, wk_ref, wvt_ref, q_ref, k_ref, vt_ref):
    tm = cq_ref.shape[0]
    scale = MLA_QK_DIM ** -0.5 * math.log2(math.e)
    cos = cos_ref[...]
    sin = sin_ref[...]
    lane = lax.broadcasted_iota(jnp.int32, (tm, LANES), 1)
    first_half = (lane % MLA_ROPE_DIM) < (MLA_ROPE_DIM // 2)
    low_head = lane < MLA_ROPE_DIM

    def rope(x):
        swapped = jnp.where(first_half,
                            pltpu.roll(x, LANES - MLA_ROPE_DIM // 2, axis=1),
                            pltpu.roll(x, MLA_ROPE_DIM // 2, axis=1))
        return x * cos + swapped * sin

    cqn = _rms_norm(cq_ref[...].astype(F32), qg_ref[...]).astype(BF16)
    qn = jnp.dot(cqn, wqn_ref[...], preferred_element_type=F32) * scale
    qr = jnp.dot(cqn, wqr_ref[...], preferred_element_type=F32)
    for pair in range(MLA_HEADS // 2):
        r = rope(qr[:, pair * LANES:(pair + 1) * LANES]) * scale
        for sub in range(2):
            h = 2 * pair + sub
            q_ref[0, h, :, :MLA_NOPE_DIM] = qn[:, h * MLA_NOPE_DIM:(h + 1) * MLA_NOPE_DIM].astype(q_ref.dtype)
            keep = low_head if sub == 0 else jnp.logical_not(low_head)
            q_ref[0, h, :, MLA_NOPE_DIM:] = jnp.where(keep, r, 0.0).astype(q_ref.dtype)

    ckvn = _rms_norm(ckv_ref[...].astype(F32), kvg_ref[...]).astype(BF16)
    kn = jnp.dot(ckvn, wk_ref[...], preferred_element_type=F32)
    vt = lax.dot_general(wvt_ref[...], ckvn, (((1,), (1,)), ((), ())), preferred_element_type=F32)
    kr = rope(kr_ref[...].astype(F32))
    kr2 = jnp.where(low_head, kr, pltpu.roll(kr, MLA_ROPE_DIM, axis=1)).astype(k_ref.dtype)
    for h in range(MLA_HEADS):
        k_ref[0, h, :, :MLA_NOPE_DIM] = kn[:, h * MLA_NOPE_DIM:(h + 1) * MLA_NOPE_DIM].astype(k_ref.dtype)
        k_ref[0, h, :, MLA_NOPE_DIM:] = kr2
        vt_ref[0, h] = vt[h * MLA_V_DIM:(h + 1) * MLA_V_DIM, :].astype(vt_ref.dtype)


def _mla_proj(proj3d, cos, sin, q_norm_g, wq_nope, wq_rope, kv_norm_g, wk_nope, wv_t):
    b, s, _ = proj3d.shape
    nb = s // TM_MLA
    tab = pl.BlockSpec((TM_MLA, LANES), lambda i, j: (i * nb + j, 0))

    def seg(width, col):
        return pl.BlockSpec((None, TM_MLA, width), lambda i, j: (i, j, col // width))

    def heads(width):
        return pl.BlockSpec((1, MLA_HEADS, TM_MLA, width), lambda i, j: (i, 0, j, 0))

    return pl.pallas_call(
        _mla_proj_kernel,
        grid=(b, nb),
        in_specs=[seg(MLA_Q_RANK, COL_CQ), seg(MLA_KV_RANK, COL_CKV), seg(LANES, COL_KR), tab, tab,
                  _resident((1, MLA_Q_RANK)), _resident(wq_nope.shape), _resident(wq_rope.shape),
                  _resident((1, MLA_KV_RANK)), _resident(wk_nope.shape), _resident(wv_t.shape)],
        out_specs=[heads(QK_PAD), heads(QK_PAD),
                   pl.BlockSpec((1, MLA_HEADS, MLA_V_DIM, TM_MLA), lambda i, j: (i, 0, 0, j))],
        out_shape=[jax.ShapeDtypeStruct((b, MLA_HEADS, s, QK_PAD), BF16),
                   jax.ShapeDtypeStruct((b, MLA_HEADS, s, QK_PAD), BF16),
                   jax.ShapeDtypeStruct((b, MLA_HEADS, MLA_V_DIM, s), BF16)],
        compiler_params=_params(),
        name="mla_proj",
    )(proj3d, proj3d, proj3d, cos, sin, q_norm_g, wq_nope, wq_rope, kv_norm_g, wk_nope, wv_t)


def _attn_kernel(q_ref, k_ref, vt_ref, o_ref, s0_ref, s1_ref):
    qi = pl.program_id(1)
    n_heads = q_ref.shape[1]
    key = lax.broadcasted_iota(jnp.int32, (TK, TQ), 0)
    qry = lax.broadcasted_iota(jnp.int32, (TK, TQ), 1)
    causal = key <= qry

    def col_max(s):
        return jnp.max(s, axis=0, keepdims=True)

    def produce(h, s_ref, nk):
        s = lax.dot_general(k_ref[0, h, :nk, :], q_ref[0, h], (((1,), (1,)), ((), ())),
                            preferred_element_type=F32)
        diag = jnp.where(causal, s[nk - TK:], -jnp.inf)
        s_ref[nk - TK:nk] = diag
        m = col_max(diag)
        if nk > TK:
            s_ref[:nk - TK] = s[:nk - TK]
            m = jnp.maximum(m, col_max(s[:nk - TK]))
        return m

    def consume(h, s_ref, m, nk):
        p = jnp.exp2(s_ref[:nk] - m)
        l = jnp.sum(p, axis=0, keepdims=True)
        o = jnp.dot(vt_ref[0, h, :, :nk], p.astype(BF16), preferred_element_type=F32) / l
        o_ref[0, h] = o.T.astype(o_ref.dtype)

    def run(nk):
        def body(hp, m_a):
            ha = 2 * hp
            m_b = produce(ha + 1, s1_ref, nk)
            consume(ha, s0_ref, m_a, nk)
            m_next = produce(ha + 2, s0_ref, nk)
            consume(ha + 1, s1_ref, m_b, nk)
            return m_next

        m_a = lax.fori_loop(0, n_heads // 2 - 1, body, produce(0, s0_ref, nk))
        m_b = produce(n_heads - 1, s1_ref, nk)
        consume(n_heads - 2, s0_ref, m_a, nk)
        consume(n_heads - 1, s1_ref, m_b, nk)

    for c in range(k_ref.shape[2] // TK):
        pl.when(qi == c)(functools.partial(run, (c + 1) * TK))


def _attention(q, k, vt):
    b, h, s, _ = q.shape
    assert TQ == TK and h % 2 == 0
    return pl.pallas_call(
        _attn_kernel,
        grid=(b, s // TQ),
        in_specs=[pl.BlockSpec((1, h, TQ, QK_PAD), lambda i, j: (i, 0, j, 0)),
                  pl.BlockSpec((1, h, s, QK_PAD), lambda i, j: (i, 0, 0, 0)),
                  pl.BlockSpec((1, h, MLA_V_DIM, s), lambda i, j: (i, 0, 0, 0))],
        out_specs=pl.BlockSpec((1, h, TQ, MLA_V_DIM), lambda i, j: (i, 0, j, 0)),
        out_shape=jax.ShapeDtypeStruct((b, h, s, MLA_V_DIM), BF16),
        scratch_shapes=[pltpu.VMEM((s, TQ), F32), pltpu.VMEM((s, TQ), F32)],
        compiler_params=_params(),
        name="mla_attention",
    )(q, k, vt)


def _sgu_kernel(uv_ref, cg_ref, ng_ref, nb_ref, ws_ref, bs_ref, o_ref):
    tm = uv_ref.shape[0]
    uv = uv_ref[...].astype(F32)
    uv = 0.5 * uv * (1.0 + lax.erf(uv * (2.0 ** -0.5)))
    u = uv[:, :SGU_WIDTH]
    vb = _layer_norm(uv[:, SGU_WIDTH:], ng_ref[...], nb_ref[...]).astype(BF16)
    gate = _silu(cg_ref[...].astype(F32))
    n_chunks = tm // SGU_CHUNK
    row = lax.broadcasted_iota(jnp.int32, (SGU_CHUNK, SGU_CHUNK), 0)
    col = lax.broadcasted_iota(jnp.int32, (SGU_CHUNK, SGU_CHUNK), 1)
    for h in range(SGU_HEADS):
        cols = slice(h * SGU_HEAD_DIM, (h + 1) * SGU_HEAD_DIM)
        w = jnp.where(col <= row, ws_ref[h], 0.0).astype(BF16)
        vh = jnp.concatenate([vb[c * SGU_CHUNK:(c + 1) * SGU_CHUNK, cols] for c in range(n_chunks)], axis=1)
        mixed = jnp.dot(w, vh, preferred_element_type=F32) + bs_ref[h]
        for c in range(n_chunks):
            rows = slice(c * SGU_CHUNK, (c + 1) * SGU_CHUNK)
            piece = mixed[:, c * SGU_HEAD_DIM:(c + 1) * SGU_HEAD_DIM]
            o_ref[rows, cols] = (u[rows, cols] * piece * gate[rows, cols]).astype(o_ref.dtype)


def _sgu(proj2d, norm_g, norm_b, w_s, b_s_col):
    t = proj2d.shape[0]
    return pl.pallas_call(
        _sgu_kernel,
        grid=(t // TM_SGU,),
        in_specs=[pl.BlockSpec((TM_SGU, 2 * SGU_WIDTH), lambda i: (i, COL_UV // (2 * SGU_WIDTH))),
                  pl.BlockSpec((TM_SGU, SGU_WIDTH), lambda i: (i, COL_CG // SGU_WIDTH)),
                  _resident((1, SGU_WIDTH)), _resident((1, SGU_WIDTH)),
                  _resident(w_s.shape), _resident(b_s_col.shape)],
        out_specs=pl.BlockSpec((TM_SGU, SGU_WIDTH), lambda i: (i, 0)),
        out_shape=jax.ShapeDtypeStruct((t, SGU_WIDTH), BF16),
        compiler_params=_params(),
        name="sgu_mixer",
    )(proj2d, proj2d, norm_g, norm_b, w_s, b_s_col)


def _out_proj_kernel(ya_ref, yb_ref, bg_ref, yc_ref, h_ref, ing_ref, inb_ref, w_ref, bo_ref,
                     pg_ref, pb_ref, o_ref, *, residual_ln):
    attn = jnp.concatenate([yb_ref[0, h] for h in range(MLA_HEADS)], axis=1).astype(F32)
    yb = (attn * _silu(bg_ref[...].astype(F32))).astype(BF16)
    y = jnp.concatenate([ya_ref[...], yb, yc_ref[...]], axis=1)
    out = jnp.dot(y, w_ref[...], preferred_element_type=F32) + bo_ref[...]
    h = h_ref[...]
    if residual_ln:
        h = _layer_norm(h, ing_ref[...], inb_ref[...])
    o_ref[...] = _layer_norm(ALPHA * h + out, pg_ref[...], pb_ref[...])


def _out_proj(y_a, y_b, proj3d, y_c, h3d, ln_in_g, ln_in_b, w_out, b_out, post_g, post_b, *, residual_ln):
    b, s, _ = h3d.shape

    def rows(width, col=0):
        return pl.BlockSpec((None, TM_OUT, width), lambda i, j: (i, j, col // width))

    vec = _resident((1, D_MODEL))
    return pl.pallas_call(
        functools.partial(_out_proj_kernel, residual_ln=residual_ln),
        grid=(b, s // TM_OUT),
        in_specs=[rows(POOL_WIDTH),
                  pl.BlockSpec((1, MLA_HEADS, TM_OUT, MLA_V_DIM), lambda i, j: (i, 0, j, 0)),
                  rows(MLA_WIDTH, COL_BG), rows(SGU_WIDTH), rows(D_MODEL),
                  vec, vec, _resident(w_out.shape), vec, vec, vec],
        out_specs=rows(D_MODEL),
        out_shape=jax.ShapeDtypeStruct((b, s, D_MODEL), F32),
        compiler_params=_params(),
        name="out_proj",
    )(y_a, y_b, proj3d, y_c, h3d, ln_in_g, ln_in_b, w_out, b_out, post_g, post_b)


def kernel(x, positions, ln_in_g, ln_in_b, w_in, pool_w, pool_scale, q_norm_g, w_uq, kv_norm_g, w_ukv,
           sgu_norm_g, sgu_norm_b, sgu_w, sgu_b, w_out, b_out, ln_post_g, ln_post_b):
    b, s, d = x.shape
    t = b * s
    assert d == D_MODEL and s % TQ == 0 and s % TM_OUT == 0 and t % TM_PROJ == 0

    cos, sin = _rope_tables(positions)
    ln_in_g2, ln_in_b2 = ln_in_g[None, :], ln_in_b[None, :]

    h = x
    for l in range(DEPTH):
        w_l = w_in[l]
        kr_end = COL_KR + MLA_ROPE_DIM
        w_a = w_l[:, :COL_KR].astype(BF16)
        w_kr = jnp.pad(w_l[:, COL_KR:kr_end], ((0, 0), (0, KR_PAD - MLA_ROPE_DIM))).astype(BF16)
        w_b = w_l[:, kr_end:].astype(BF16)
        wq = w_uq[l].reshape(MLA_Q_RANK, MLA_HEADS, MLA_QK_DIM)
        wq_nope = wq[:, :, :MLA_NOPE_DIM].reshape(MLA_Q_RANK, MLA_HEADS * MLA_NOPE_DIM).astype(BF16)
        wq_rope = wq[:, :, MLA_NOPE_DIM:].reshape(MLA_Q_RANK, MLA_HEADS * MLA_ROPE_DIM).astype(BF16)

        first = l == 0
        proj = _in_proj(h.reshape(t, d), ln_in_g2, ln_in_b2, w_a, w_kr, w_b, apply_ln=first)
        proj3d = proj.reshape(b, s, PROJ_COLS)

        y_a = _pool_mixer(proj3d, pool_w[l].astype(BF16), pool_scale[l][None, :])
        wkv = w_ukv[l].reshape(MLA_KV_RANK, MLA_HEADS, MLA_NOPE_DIM + MLA_V_DIM)
        wk_nope = wkv[:, :, :MLA_NOPE_DIM].reshape(MLA_KV_RANK, MLA_HEADS * MLA_NOPE_DIM).astype(BF16)
        wv_t = wkv[:, :, MLA_NOPE_DIM:].reshape(MLA_KV_RANK, MLA_HEADS * MLA_V_DIM).T.astype(BF16)
        q, k, vt = _mla_proj(proj3d, cos, sin, q_norm_g[l][None, :], wq_nope, wq_rope,
                             kv_norm_g[l][None, :], wk_nope, wv_t)
        y_b = _attention(q, k, vt)
        y_c = _sgu(proj, sgu_norm_g[l][None, :], sgu_norm_b[l][None, :], sgu_w[l], sgu_b[l][:, :, None])
        h = _out_proj(y_a, y_b, proj3d, y_c.reshape(b, s, SGU_WIDTH), h, ln_in_g2, ln_in_b2,
                      w_out[l].astype(BF16), b_out[l][None, :], ln_post_g[l][None, :], ln_post_b[l][None, :],
                      residual_ln=first)
    return h
```

```python
import functools
import math

import jax
import jax.numpy as jnp
from jax import lax
from jax.experimental import pallas as pl
from jax.experimental.pallas import tpu as pltpu

F32 = jnp.float32
BF16 = jnp.bfloat16

D_MODEL = 2048
DEPTH = 2
EPS = 1e-5
POOL_WIDTH = 512
POOL_WINDOWS = (2, 4, 8, 16)
POOL_GROUP_DIM = 128
MLA_HEADS = 8
MLA_NOPE_DIM = 128
MLA_ROPE_DIM = 64
MLA_V_DIM = 128
MLA_WIDTH = MLA_HEADS * MLA_V_DIM
MLA_Q_RANK = 512
MLA_KV_RANK = 256
MLA_QK_DIM = MLA_NOPE_DIM + MLA_ROPE_DIM
ROPE_THETA = 10000.0
SGU_WIDTH = 512
SGU_HEADS = 4
SGU_HEAD_DIM = 128
SGU_CHUNK = 128
ALPHA = (2.0 * DEPTH) ** 0.25

LANES = 128
MXU_DIM = 256
VMEM_LIMIT = 56 * 1024 * 1024

SRC_AX = 0
SRC_AG = SRC_AX + POOL_WIDTH
SRC_CQ = SRC_AG + POOL_WIDTH
SRC_CKV = SRC_CQ + MLA_Q_RANK
SRC_KR = SRC_CKV + MLA_KV_RANK
SRC_BG = SRC_KR + MLA_ROPE_DIM
SRC_UV = SRC_BG + MLA_WIDTH
SRC_CG = SRC_UV + 2 * SGU_WIDTH
SRC_END = SRC_CG + SGU_WIDTH
LAT_CQ = 0
LAT_CKV = LAT_CQ + MLA_Q_RANK
LAT_KR = LAT_CKV + MLA_KV_RANK
LAT_COLS = LAT_KR + MXU_DIM
MIX_UV = 0
MIX_CG = MIX_UV + 2 * SGU_WIDTH
MIX_AX = MIX_CG + SGU_WIDTH
MIX_AG = MIX_AX + POOL_WIDTH
MIX_COLS = MIX_AG + POOL_WIDTH
QK_PAD = MXU_DIM
POOL_HALO = 16

TM_FRONT = 512
TM_MLA = 512
TQ = 512
TK = 512
TM_OUT = 512


def _layer_norm(x, g, b):
    mu = jnp.mean(x, axis=-1, keepdims=True)
    xc = x - mu
    var = jnp.mean(xc * xc, axis=-1, keepdims=True)
    return xc * lax.rsqrt(var + EPS) * g + b


def _rms_norm(x, g):
    ms = jnp.mean(x * x, axis=-1, keepdims=True)
    return x * lax.rsqrt(ms + EPS) * g


def _silu(x):
    return x / (1.0 + jnp.exp(-x))


def _resident(shape):
    return pl.BlockSpec(shape, lambda *_: (0,) * len(shape), pipeline_mode=pl.Buffered(1))


def _layer(stacked, l):
    tail = stacked.shape[1:]
    return pl.BlockSpec((None,) + tail, lambda *_: (l,) + (0,) * len(tail), pipeline_mode=pl.Buffered(1))


def _params():
    return pltpu.CompilerParams(vmem_limit_bytes=VMEM_LIMIT)


def _rope_table_kernel(pos_ref, freq_ref, sign_ref, cos_ref, sin_ref):
    ang = pos_ref[...].astype(F32) * freq_ref[...]
    cos_ref[...] = jnp.cos(ang)
    sin_ref[...] = jnp.sin(ang) * sign_ref[...]


def _rope_tables(positions):
    t = positions.size
    half = MLA_ROPE_DIM // 2
    inv_freq = ROPE_THETA ** (-jnp.arange(half, dtype=F32) / half)
    freq_row = jnp.tile(inv_freq, LANES // half)[None, :]
    sign_row = jnp.tile(jnp.concatenate([-jnp.ones(half, F32), jnp.ones(half, F32)]),
                        LANES // MLA_ROPE_DIM)[None, :]
    tm = 2048
    row = pl.BlockSpec((tm, LANES), lambda i: (i, 0))
    return pl.pallas_call(
        _rope_table_kernel,
        grid=(t // tm,),
        in_specs=[pl.BlockSpec((tm, 1), lambda i: (i, 0)), _resident((1, LANES)), _resident((1, LANES))],
        out_specs=[row, row],
        out_shape=[jax.ShapeDtypeStruct((t, LANES), F32)] * 2,
        compiler_params=_params(),
        name="rope_tables",
    )(positions.reshape(t, 1), freq_row, sign_row)


def _pool_mixer(ax, ag, seq_tile, halo_ref, w_ref, sc_ref, o_ref):
    tm = ax.shape[0]

    @pl.when(seq_tile == 0)
    def _():
        halo_ref[...] = jnp.zeros_like(halo_ref)

    halo = halo_ref[...]
    t = seq_tile * tm + lax.broadcasted_iota(jnp.int32, (tm, POOL_GROUP_DIM), 0)
    for gi, win in enumerate(POOL_WINDOWS):
        cols = slice(gi * POOL_GROUP_DIM, (gi + 1) * POOL_GROUP_DIM)
        x = ax[:, cols]
        acc = jnp.concatenate([halo[:, cols], x], axis=0)
        span = 1
        while span < win:
            acc = acc + pltpu.roll(acc, span, axis=0)
            span *= 2
        count = jnp.minimum(t + 1, win).astype(F32)
        d = (acc[POOL_HALO:] / count - x).astype(BF16)
        y = jnp.dot(d, w_ref[gi], preferred_element_type=F32)
        o_ref[:, cols] = (y * sc_ref[:, cols] * _silu(ag[:, cols])).astype(o_ref.dtype)
    halo_ref[...] = ax[tm - POOL_HALO:, :]


def _sgu_mixer(uv, cg, ng_ref, nb_ref, ws_ref, bs_ref, o_ref):
    tm = uv.shape[0]
    uv = 0.5 * uv * (1.0 + lax.erf(uv * (2.0 ** -0.5)))
    u = uv[:, :SGU_WIDTH]
    vb = _layer_norm(uv[:, SGU_WIDTH:], ng_ref[...], nb_ref[...]).astype(BF16)
    gate = _silu(cg)
    n_chunks = tm // SGU_CHUNK
    row = lax.broadcasted_iota(jnp.int32, (SGU_CHUNK, SGU_CHUNK), 0)
    col = lax.broadcasted_iota(jnp.int32, (SGU_CHUNK, SGU_CHUNK), 1)
    for h in range(SGU_HEADS):
        cols = slice(h * SGU_HEAD_DIM, (h + 1) * SGU_HEAD_DIM)
        w = jnp.where(col <= row, ws_ref[h], 0.0).astype(BF16)
        vh = jnp.concatenate([vb[c * SGU_CHUNK:(c + 1) * SGU_CHUNK, cols] for c in range(n_chunks)], axis=1)
        mixed = jnp.dot(w, vh, preferred_element_type=F32) + bs_ref[h]
        for c in range(n_chunks):
            rows = slice(c * SGU_CHUNK, (c + 1) * SGU_CHUNK)
            piece = mixed[:, c * SGU_HEAD_DIM:(c + 1) * SGU_HEAD_DIM]
            o_ref[rows, cols] = (u[rows, cols] * piece * gate[rows, cols]).astype(o_ref.dtype)


def _front_kernel(x_ref, g_ref, b_ref, wlat_ref, wmix_ref, wbg_ref, pw_ref, psc_ref, ng_ref, nb_ref,
                  ws_ref, bs_ref, lat_ref, bg_ref, ya_ref, yc_ref, halo_ref, *, apply_ln):
    x = x_ref[...]
    if apply_ln:
        x = _layer_norm(x, g_ref[...], b_ref[...])
    xb = x.astype(BF16)

    def proj(w_ref, lo, hi):
        return jnp.dot(xb, w_ref[:, lo:hi], preferred_element_type=F32)

    uv, cg = proj(wmix_ref, MIX_UV, MIX_CG), proj(wmix_ref, MIX_CG, MIX_AX)
    ax, ag = proj(wmix_ref, MIX_AX, MIX_AG), proj(wmix_ref, MIX_AG, MIX_COLS)
    lat_ref[...] = proj(wlat_ref, 0, LAT_COLS).astype(lat_ref.dtype)
    _sgu_mixer(uv, cg, ng_ref, nb_ref, ws_ref, bs_ref, yc_ref)
    _pool_mixer(ax, ag, pl.program_id(1), halo_ref, pw_ref, psc_ref, ya_ref)
    bg_ref[...] = proj(wbg_ref, 0, MLA_WIDTH).astype(bg_ref.dtype)


def _front(x3d, ln_g, ln_b, w_lat, w_mix, w_bg, w_pool, pool_scale, sgu_g, sgu_b, w_s, b_s_col, l, *, apply_ln):
    b, s, _ = x3d.shape
    assert w_lat.shape[2] == LAT_COLS and w_mix.shape[2] == MIX_COLS and w_bg.shape[2] == MLA_WIDTH
    assert TM_FRONT % SGU_CHUNK == 0 and POOL_HALO >= max(POOL_WINDOWS) - 1
    stacks = (w_lat, w_mix, w_bg, w_pool, pool_scale, sgu_g, sgu_b, w_s, b_s_col)

    def rows(width):
        return pl.BlockSpec((None, TM_FRONT, width), lambda i, j: (i, j, 0))

    def out(width):
        return jax.ShapeDtypeStruct((b, s, width), BF16)

    return pl.pallas_call(
        functools.partial(_front_kernel, apply_ln=apply_ln),
        grid=(b, s // TM_FRONT),
        in_specs=[rows(D_MODEL), _resident((1, D_MODEL)), _resident((1, D_MODEL))]
                 + [_layer(p, l) for p in stacks],
        out_specs=[rows(LAT_COLS), rows(MLA_WIDTH), rows(POOL_WIDTH), rows(SGU_WIDTH)],
        out_shape=[out(LAT_COLS), out(MLA_WIDTH), out(POOL_WIDTH), out(SGU_WIDTH)],
        scratch_shapes=[pltpu.VMEM((POOL_HALO, POOL_WIDTH), F32)],
        compiler_params=pltpu.CompilerParams(vmem_limit_bytes=VMEM_LIMIT,
                                             dimension_semantics=("arbitrary", "arbitrary")),
        name="front",
    )(x3d, ln_g, ln_b, *stacks)


def _mla_proj_kernel(cq_ref, ckv_ref, kr_ref, cos_ref, sin_ref, qg_ref, wqn_ref, wqr_ref,
                     kvg_ref, wk_ref, wvt_ref, q_ref, k_ref, vt_ref):
    tm = cq_ref.shape[0]
    scale = MLA_QK_DIM ** -0.5 * math.log2(math.e)
    cos = cos_ref[...]
    sin = sin_ref[...]
    lane = lax.broadcasted_iota(jnp.int32, (tm, LANES), 1)
    first_half = (lane % MLA_ROPE_DIM) < (MLA_ROPE_DIM // 2)
    low_head = lane < MLA_ROPE_DIM

    def rope(x):
        swapped = jnp.where(first_half,
                            pltpu.roll(x, LANES - MLA_ROPE_DIM // 2, axis=1),
                            pltpu.roll(x, MLA_ROPE_DIM // 2, axis=1))
        return x * cos + swapped * sin

    cqn = _rms_norm(cq_ref[...].astype(F32), qg_ref[...]).astype(BF16)
    qn = jnp.dot(cqn, wqn_ref[...], preferred_element_type=F32) * scale
    qr = jnp.dot(cqn, wqr_ref[...], preferred_element_type=F32)
    for pair in range(MLA_HEADS // 2):
        r = rope(qr[:, pair * LANES:(pair + 1) * LANES]) * scale
        for sub in range(2):
            h = 2 * pair + sub
            q_ref[0, h, :, :MLA_NOPE_DIM] = qn[:, h * MLA_NOPE_DIM:(h + 1) * MLA_NOPE_DIM].astype(q_ref.dtype)
            keep = low_head if sub == 0 else jnp.logical_not(low_head)
            q_ref[0, h, :, MLA_NOPE_DIM:] = jnp.where(keep, r, 0.0).astype(q_ref.dtype)

    ckvn = _rms_norm(ckv_ref[...].astype(F32), kvg_ref[...]).astype(BF16)
    kn = jnp.dot(ckvn, wk_ref[...], preferred_element_type=F32)
    vt = lax.dot_general(wvt_ref[...], ckvn, (((1,), (1,)), ((), ())), preferred_element_type=F32)
    kr = rope(kr_ref[...].astype(F32))
    kr2 = jnp.where(low_head, kr, pltpu.roll(kr, MLA_ROPE_DIM, axis=1)).astype(k_ref.dtype)
    for h in range(MLA_HEADS):
        k_ref[0, h, :, :MLA_NOPE_DIM] = kn[:, h * MLA_NOPE_DIM:(h + 1) * MLA_NOPE_DIM].astype(k_ref.dtype)
        k_ref[0, h, :, MLA_NOPE_DIM:] = kr2
        vt_ref[0, h] = vt[h * MLA_V_DIM:(h + 1) * MLA_V_DIM, :].astype(vt_ref.dtype)


def _mla_proj(lat3d, cos, sin, q_norm_g, wq_nope, wq_rope, kv_norm_g, wk_nope, wv_t, l):
    b, s, _ = lat3d.shape
    nb = s // TM_MLA
    stacks = (q_norm_g, wq_nope, wq_rope, kv_norm_g, wk_nope, wv_t)
    tab = pl.BlockSpec((TM_MLA, LANES), lambda i, j: (i * nb + j, 0))

    def seg(width, col):
        return pl.BlockSpec((None, TM_MLA, width), lambda i, j: (i, j, col // width))

    def heads(width):
        return pl.BlockSpec((1, MLA_HEADS, TM_MLA, width), lambda i, j: (i, 0, j, 0))

    return pl.pallas_call(
        _mla_proj_kernel,
        grid=(b, nb),
        in_specs=[seg(MLA_Q_RANK, LAT_CQ), seg(MLA_KV_RANK, LAT_CKV), seg(LANES, LAT_KR), tab, tab]
                 + [_layer(p, l) for p in stacks],
        out_specs=[heads(QK_PAD), heads(QK_PAD),
                   pl.BlockSpec((1, MLA_HEADS, MLA_V_DIM, TM_MLA), lambda i, j: (i, 0, 0, j))],
        out_shape=[jax.ShapeDtypeStruct((b, MLA_HEADS, s, QK_PAD), BF16),
                   jax.ShapeDtypeStruct((b, MLA_HEADS, s, QK_PAD), BF16),
                   jax.ShapeDtypeStruct((b, MLA_HEADS, MLA_V_DIM, s), BF16)],
        compiler_params=_params(),
        name="mla_proj",
    )(lat3d, lat3d, lat3d, cos, sin, *stacks)


def _attn_kernel(q_ref, k_ref, vt_ref, o_ref, s0_ref, s1_ref):
    qi = pl.program_id(1)
    n_heads = q_ref.shape[1]
    hk = TK // 2
    nt_dims = (((1,), (1,)), ((), ()))
    key = lax.broadcasted_iota(jnp.int32, (hk, TQ), 0)
    qry = lax.broadcasted_iota(jnp.int32, (hk, TQ), 1)
    causal = key <= qry

    def col_max(s):
        return jnp.max(s, axis=0, keepdims=True)

    def produce(h, s_ref, nk):
        s = lax.dot_general(k_ref[0, h, :nk - hk, :], q_ref[0, h], nt_dims,
                            preferred_element_type=F32)
        s_low = lax.dot_general(k_ref[0, h, nk - hk:nk, :], q_ref[0, h, hk:, :], nt_dims,
                                preferred_element_type=F32)
        top = jnp.where(causal, s[nk - TK:], -jnp.inf)
        low = jnp.where(causal[:, :hk], s_low, -jnp.inf)
        s_ref[nk - TK:nk - hk] = top
        s_ref[nk - hk:nk, hk:] = low
        m = col_max(top)
        if nk > TK:
            s_ref[:nk - TK] = s[:nk - TK]
            m = jnp.maximum(m, col_max(s[:nk - TK]))
        return jnp.concatenate([m[:, :hk], jnp.maximum(m[:, hk:], col_max(low))], axis=1)

    def consume(h, s_ref, m, nk):
        p = jnp.exp2(s_ref[:nk - hk] - m)
        p_low = jnp.exp2(s_ref[nk - hk:nk, hk:] - m[:, hk:])
        l = jnp.sum(p, axis=0, keepdims=True)
        l = jnp.concatenate([l[:, :hk], l[:, hk:] + jnp.sum(p_low, axis=0, keepdims=True)], axis=1)
        pb = p.astype(BF16)
        o_first = jnp.dot(vt_ref[0, h, :, :nk - hk], pb[:, :hk], preferred_element_type=F32)
        o_last = jnp.dot(vt_ref[0, h, :, :nk], jnp.concatenate([pb[:, hk:], p_low.astype(BF16)], axis=0),
                         preferred_element_type=F32)
        o = jnp.concatenate([o_first, o_last], axis=1) / l
        o_ref[0, h] = o.T.astype(o_ref.dtype)

    def run(nk):
        def body(hp, m_a):
            ha = 2 * hp
            m_b = produce(ha + 1, s1_ref, nk)
            consume(ha, s0_ref, m_a, nk)
            m_next = produce(ha + 2, s0_ref, nk)
            consume(ha + 1, s1_ref, m_b, nk)
            return m_next

        m_a = lax.fori_loop(0, n_heads // 2 - 1, body, produce(0, s0_ref, nk))
        m_b = produce(n_heads - 1, s1_ref, nk)
        consume(n_heads - 2, s0_ref, m_a, nk)
        consume(n_heads - 1, s1_ref, m_b, nk)

    for c in range(k_ref.shape[2] // TK):
        pl.when(qi == c)(functools.partial(run, (c + 1) * TK))


def _attention(q, k, vt):
    b, h, s, _ = q.shape
    assert TQ == TK and h % 2 == 0
    return pl.pallas_call(
        _attn_kernel,
        grid=(b, s // TQ),
        in_specs=[pl.BlockSpec((1, h, TQ, QK_PAD), lambda i, j: (i, 0, j, 0)),
                  pl.BlockSpec((1, h, s, QK_PAD), lambda i, j: (i, 0, 0, 0)),
                  pl.BlockSpec((1, h, MLA_V_DIM, s), lambda i, j: (i, 0, 0, 0))],
        out_specs=pl.BlockSpec((1, h, TQ, MLA_V_DIM), lambda i, j: (i, 0, j, 0)),
        out_shape=jax.ShapeDtypeStruct((b, h, s, MLA_V_DIM), BF16),
        scratch_shapes=[pltpu.VMEM((s, TQ), F32), pltpu.VMEM((s, TQ), F32)],
        compiler_params=_params(),
        name="mla_attention",
    )(q, k, vt)


def _out_proj_kernel(ya_ref, yb_ref, bg_ref, yc_ref, h_ref, ing_ref, inb_ref, w_ref, bo_ref,
                     pg_ref, pb_ref, o_ref, *, residual_ln):
    attn = jnp.concatenate([yb_ref[0, h] for h in range(MLA_HEADS)], axis=1).astype(F32)
    yb = (attn * _silu(bg_ref[...].astype(F32))).astype(BF16)
    y = jnp.concatenate([ya_ref[...], yb, yc_ref[...]], axis=1)
    out = jnp.dot(y, w_ref[...], preferred_element_type=F32) + bo_ref[...]
    h = h_ref[...]
    if residual_ln:
        h = _layer_norm(h, ing_ref[...], inb_ref[...])
    o_ref[...] = _layer_norm(ALPHA * h + out, pg_ref[...], pb_ref[...])


def _out_proj(y_a, y_b, b_gate, y_c, h3d, ln_in_g, ln_in_b, w_out, b_out, post_g, post_b, l, *, residual_ln):
    b, s, _ = h3d.shape

    def rows(width):
        return pl.BlockSpec((None, TM_OUT, width), lambda i, j: (i, j, 0))

    vec = _resident((1, D_MODEL))
    return pl.pallas_call(
        functools.partial(_out_proj_kernel, residual_ln=residual_ln),
        grid=(b, s // TM_OUT),
        in_specs=[rows(POOL_WIDTH),
                  pl.BlockSpec((1, MLA_HEADS, TM_OUT, MLA_V_DIM), lambda i, j: (i, 0, j, 0)),
                  rows(MLA_WIDTH), rows(SGU_WIDTH), rows(D_MODEL),
                  vec, vec, _layer(w_out, l), _layer(b_out, l), _layer(post_g, l), _layer(post_b, l)],
        out_specs=rows(D_MODEL),
        out_shape=jax.ShapeDtypeStruct((b, s, D_MODEL), F32),
        compiler_params=_params(),
        name="out_proj",
    )(y_a, y_b, b_gate, y_c, h3d, ln_in_g, ln_in_b, w_out, b_out, post_g, post_b)


def kernel(x, positions, ln_in_g, ln_in_b, w_in, pool_w, pool_scale, q_norm_g, w_uq, kv_norm_g, w_ukv,
           sgu_norm_g, sgu_norm_b, sgu_w, sgu_b, w_out, b_out, ln_post_g, ln_post_b):
    b, s, d = x.shape
    assert d == D_MODEL and s % TQ == 0 and s % TM_OUT == 0 and s % TM_FRONT == 0 and s % TM_MLA == 0

    cos, sin = _rope_tables(positions)
    ln_in_g2, ln_in_b2 = ln_in_g[None, :], ln_in_b[None, :]

    depth = w_in.shape[0]
    assert depth == DEPTH
    w_lat = jnp.pad(w_in[:, :, SRC_CQ:SRC_BG], ((0, 0), (0, 0), (0, LAT_COLS - (SRC_BG - SRC_CQ)))).astype(BF16)
    w_mix = jnp.concatenate([w_in[:, :, SRC_UV:SRC_END], w_in[:, :, SRC_AX:SRC_CQ]], axis=2).astype(BF16)
    w_bg = w_in[:, :, SRC_BG:SRC_UV].astype(BF16)
    wq = w_uq.reshape(depth, MLA_Q_RANK, MLA_HEADS, MLA_QK_DIM)
    wq_nope = wq[..., :MLA_NOPE_DIM].reshape(depth, MLA_Q_RANK, MLA_HEADS * MLA_NOPE_DIM).astype(BF16)
    wq_rope = wq[..., MLA_NOPE_DIM:].reshape(depth, MLA_Q_RANK, MLA_HEADS * MLA_ROPE_DIM).astype(BF16)
    wkv = w_ukv.reshape(depth, MLA_KV_RANK, MLA_HEADS, MLA_NOPE_DIM + MLA_V_DIM)
    wk_nope = wkv[..., :MLA_NOPE_DIM].reshape(depth, MLA_KV_RANK, MLA_HEADS * MLA_NOPE_DIM).astype(BF16)
    wv_t = wkv[..., MLA_NOPE_DIM:].reshape(depth, MLA_KV_RANK, MLA_HEADS * MLA_V_DIM).swapaxes(1, 2).astype(BF16)
    w_pool = pool_w.astype(BF16)
    w_out_b = w_out.astype(BF16)

    def row(p):
        return p[:, None, :]

    h = x
    for l in range(depth):
        first = l == 0
        lat, b_gate, y_a, y_c = _front(
            h, ln_in_g2, ln_in_b2, w_lat, w_mix, w_bg, w_pool, row(pool_scale),
            row(sgu_norm_g), row(sgu_norm_b), sgu_w, sgu_b[..., None], l, apply_ln=first)
        q, k, vt = _mla_proj(lat, cos, sin, row(q_norm_g), wq_nope, wq_rope, row(kv_norm_g), wk_nope, wv_t, l)
        y_b = _attention(q, k, vt)
        h = _out_proj(y_a, y_b, b_gate, y_c, h, ln_in_g2, ln_in_b2,
                      w_out_b, row(b_out), row(ln_post_g), row(ln_post_b), l, residual_ln=first)
    return h
```

```python
import functools
import math

import jax
import jax.numpy as jnp
from jax import lax
from jax.experimental import pallas as pl
from jax.experimental.pallas import tpu as pltpu

F32 = jnp.float32
BF16 = jnp.bfloat16

D_MODEL = 2048
DEPTH = 2
EPS = 1e-5
POOL_WIDTH = 512
POOL_WINDOWS = (2, 4, 8, 16)
POOL_GROUP_DIM = 128
MLA_HEADS = 8
MLA_NOPE_DIM = 128
MLA_ROPE_DIM = 64
MLA_V_DIM = 128
MLA_WIDTH = MLA_HEADS * MLA_V_DIM
MLA_Q_RANK = 512
MLA_KV_RANK = 256
MLA_QK_DIM = MLA_NOPE_DIM + MLA_ROPE_DIM
ROPE_THETA = 10000.0
SGU_WIDTH = 512
SGU_HEADS = 4
SGU_HEAD_DIM = 128
SGU_CHUNK = 128
ALPHA = (2.0 * DEPTH) ** 0.25

LANES = 128
MXU_DIM = 256
VMEM_LIMIT = 56 * 1024 * 1024

SRC_AX = 0
SRC_AG = SRC_AX + POOL_WIDTH
SRC_CQ = SRC_AG + POOL_WIDTH
SRC_CKV = SRC_CQ + MLA_Q_RANK
SRC_KR = SRC_CKV + MLA_KV_RANK
SRC_BG = SRC_KR + MLA_ROPE_DIM
SRC_UV = SRC_BG + MLA_WIDTH
SRC_CG = SRC_UV + 2 * SGU_WIDTH
SRC_END = SRC_CG + SGU_WIDTH
LAT_CQ = 0
LAT_CKV = LAT_CQ + MLA_Q_RANK
LAT_KR = LAT_CKV + MLA_KV_RANK
LAT_COLS = LAT_KR + MXU_DIM
MIX_UV = 0
MIX_CG = MIX_UV + 2 * SGU_WIDTH
MIX_AX = MIX_CG + SGU_WIDTH
MIX_AG = MIX_AX + POOL_WIDTH
MIX_COLS = MIX_AG + POOL_WIDTH
QK_PAD = MXU_DIM
POOL_HALO = 16

TR_PREP = 256
TM_FRONT = 512
TM_MLA = 512
TQ = 512
TK = 512
TM_OUT = 512


def _layer_norm(x, g, b):
    mu = jnp.mean(x, axis=-1, keepdims=True)
    xc = x - mu
    var = jnp.mean(xc * xc, axis=-1, keepdims=True)
    return xc * lax.rsqrt(var + EPS) * g + b


def _rms_norm(x, g):
    ms = jnp.mean(x * x, axis=-1, keepdims=True)
    return x * lax.rsqrt(ms + EPS) * g


def _silu(x):
    return x / (1.0 + jnp.exp(-x))


def _resident(shape):
    return pl.BlockSpec(shape, lambda *_: (0,) * len(shape), pipeline_mode=pl.Buffered(1))


def _layer(stacked, l):
    tail = stacked.shape[1:]
    return pl.BlockSpec((None,) + tail, lambda *_: (l,) + (0,) * len(tail), pipeline_mode=pl.Buffered(1))


def _params():
    return pltpu.CompilerParams(vmem_limit_bytes=VMEM_LIMIT)


def _rope_table_kernel(pos_ref, freq_ref, cos_ref, sin_ref):
    ang = pos_ref[...].astype(F32) * freq_ref[...]
    cos_ref[...] = jnp.cos(ang)
    sin_ref[...] = jnp.sin(ang)


def _rope_tables(positions):
    t = positions.size
    half = MLA_ROPE_DIM // 2
    per_row = LANES // half
    inv_freq = ROPE_THETA ** (-jnp.arange(half, dtype=F32) / half)
    freq_row = jnp.tile(inv_freq, per_row)[None, :]
    pos_rows = jnp.repeat(positions.reshape(t // per_row, per_row), half, axis=1)
    tm = 1024
    row = pl.BlockSpec((tm, LANES), lambda i: (i, 0))
    cos, sin = pl.pallas_call(
        _rope_table_kernel,
        grid=(t // per_row // tm,),
        in_specs=[row, _resident((1, LANES))],
        out_specs=[row, row],
        out_shape=[jax.ShapeDtypeStruct((t // per_row, LANES), F32)] * 2,
        compiler_params=_params(),
        name="rope_tables",
    )(pos_rows, freq_row)
    return (jnp.tile(cos.reshape(t, half), (1, per_row)), jnp.tile(sin.reshape(t, half), (1, per_row)))


def _regroup_kernel(w_ref, lat_ref, mix_ref, bg_ref):
    w = w_ref[...]
    n_lat = SRC_BG - SRC_CQ
    lat_ref[:, :n_lat] = w[:, SRC_CQ:SRC_BG].astype(lat_ref.dtype)
    lat_ref[:, n_lat:] = jnp.zeros((w.shape[0], LAT_COLS - n_lat), lat_ref.dtype)
    mix_ref[:, :MIX_AX] = w[:, SRC_UV:SRC_END].astype(mix_ref.dtype)
    mix_ref[:, MIX_AX:] = w[:, SRC_AX:SRC_CQ].astype(mix_ref.dtype)
    bg_ref[...] = w[:, SRC_BG:SRC_UV].astype(bg_ref.dtype)


def _regroup_in_proj(w_in):
    depth, d, n = w_in.shape
    assert n == SRC_END

    def rows(width):
        return pl.BlockSpec((None, TR_PREP, width), lambda l, i: (l, i, 0))

    def out(width):
        return jax.ShapeDtypeStruct((depth, d, width), BF16)

    return pl.pallas_call(
        _regroup_kernel,
        grid=(depth, d // TR_PREP),
        in_specs=[rows(n)],
        out_specs=[rows(LAT_COLS), rows(MIX_COLS), rows(MLA_WIDTH)],
        out_shape=[out(LAT_COLS), out(MIX_COLS), out(MLA_WIDTH)],
        compiler_params=_params(),
        name="regroup_in_proj",
    )(w_in)


def _pool_mixer(ax, ag, seq_tile, halo_ref, w_ref, sc_ref, o_ref):
    tm = ax.shape[0]

    @pl.when(seq_tile == 0)
    def _():
        halo_ref[...] = jnp.zeros_like(halo_ref)

    halo = halo_ref[...]
    t = seq_tile * tm + lax.broadcasted_iota(jnp.int32, (tm, POOL_GROUP_DIM), 0)
    for gi, win in enumerate(POOL_WINDOWS):
        cols = slice(gi * POOL_GROUP_DIM, (gi + 1) * POOL_GROUP_DIM)
        x = ax[:, cols]
        acc = jnp.concatenate([halo[:, cols], x], axis=0)
        span = 1
        while span < win:
            acc = acc + pltpu.roll(acc, span, axis=0)
            span *= 2
        count = jnp.minimum(t + 1, win).astype(F32)
        d = (acc[POOL_HALO:] / count - x).astype(BF16)
        y = jnp.dot(d, w_ref[gi], preferred_element_type=F32)
        o_ref[:, cols] = (y * sc_ref[:, cols] * _silu(ag[:, cols])).astype(o_ref.dtype)
    halo_ref[...] = ax[tm - POOL_HALO:, :]


def _sgu_mixer(uv, cg, ng_ref, nb_ref, ws_ref, bs_ref, o_ref):
    tm = uv.shape[0]
    uv = 0.5 * uv * (1.0 + lax.erf(uv * (2.0 ** -0.5)))
    u = uv[:, :SGU_WIDTH]
    vb = _layer_norm(uv[:, SGU_WIDTH:], ng_ref[...], nb_ref[...]).astype(BF16)
    gate = _silu(cg)
    n_chunks = tm // SGU_CHUNK
    row = lax.broadcasted_iota(jnp.int32, (SGU_CHUNK, SGU_CHUNK), 0)
    col = lax.broadcasted_iota(jnp.int32, (SGU_CHUNK, SGU_CHUNK), 1)
    for h in range(SGU_HEADS):
        cols = slice(h * SGU_HEAD_DIM, (h + 1) * SGU_HEAD_DIM)
        w = jnp.where(col <= row, ws_ref[h], 0.0).astype(BF16)
        vh = jnp.concatenate([vb[c * SGU_CHUNK:(c + 1) * SGU_CHUNK, cols] for c in range(n_chunks)], axis=1)
        mixed = jnp.dot(w, vh, preferred_element_type=F32) + bs_ref[h]
        for c in range(n_chunks):
            rows = slice(c * SGU_CHUNK, (c + 1) * SGU_CHUNK)
            piece = mixed[:, c * SGU_HEAD_DIM:(c + 1) * SGU_HEAD_DIM]
            o_ref[rows, cols] = (u[rows, cols] * piece * gate[rows, cols]).astype(o_ref.dtype)


def _front_kernel(x_ref, g_ref, b_ref, wlat_ref, wmix_ref, wbg_ref, pw_ref, psc_ref, ng_ref, nb_ref,
                  ws_ref, bs_ref, lat_ref, bg_ref, ya_ref, yc_ref, halo_ref, *, apply_ln):
    x = x_ref[...]
    if apply_ln:
        x = _layer_norm(x, g_ref[...], b_ref[...])
    xb = x.astype(BF16)

    def proj(w_ref, lo, hi):
        return jnp.dot(xb, w_ref[:, lo:hi], preferred_element_type=F32)

    uv, cg = proj(wmix_ref, MIX_UV, MIX_CG), proj(wmix_ref, MIX_CG, MIX_AX)
    ax, ag = proj(wmix_ref, MIX_AX, MIX_AG), proj(wmix_ref, MIX_AG, MIX_COLS)
    lat_ref[...] = proj(wlat_ref, 0, LAT_COLS).astype(lat_ref.dtype)
    _sgu_mixer(uv, cg, ng_ref, nb_ref, ws_ref, bs_ref, yc_ref)
    _pool_mixer(ax, ag, pl.program_id(1), halo_ref, pw_ref, psc_ref, ya_ref)
    bg_ref[...] = proj(wbg_ref, 0, MLA_WIDTH).astype(bg_ref.dtype)


def _front(x3d, ln_g, ln_b, w_lat, w_mix, w_bg, w_pool, pool_scale, sgu_g, sgu_b, w_s, b_s_col, l, *, apply_ln):
    b, s, _ = x3d.shape
    assert w_lat.shape[2] == LAT_COLS and w_mix.shape[2] == MIX_COLS and w_bg.shape[2] == MLA_WIDTH
    assert TM_FRONT % SGU_CHUNK == 0 and POOL_HALO >= max(POOL_WINDOWS) - 1
    stacks = (w_lat, w_mix, w_bg, w_pool, pool_scale, sgu_g, sgu_b, w_s, b_s_col)

    def rows(width):
        return pl.BlockSpec((None, TM_FRONT, width), lambda i, j: (i, j, 0))

    def out(width):
        return jax.ShapeDtypeStruct((b, s, width), BF16)

    return pl.pallas_call(
        functools.partial(_front_kernel, apply_ln=apply_ln),
        grid=(b, s // TM_FRONT),
        in_specs=[rows(D_MODEL), _resident((1, D_MODEL)), _resident((1, D_MODEL))]
                 + [_layer(p, l) for p in stacks],
        out_specs=[rows(LAT_COLS), rows(MLA_WIDTH), rows(POOL_WIDTH), rows(SGU_WIDTH)],
        out_shape=[out(LAT_COLS), out(MLA_WIDTH), out(POOL_WIDTH), out(SGU_WIDTH)],
        scratch_shapes=[pltpu.VMEM((POOL_HALO, POOL_WIDTH), F32)],
        compiler_params=pltpu.CompilerParams(vmem_limit_bytes=VMEM_LIMIT,
                                             dimension_semantics=("arbitrary", "arbitrary")),
        name="front",
    )(x3d, ln_g, ln_b, *stacks)


def _mla_proj_kernel(cq_ref, ckv_ref, kr_ref, cos_ref, sin_ref, qg_ref, wqn_ref, wqr_ref,
                     kvg_ref, wk_ref, wvt_ref, q_ref, k_ref, vt_ref):
    tm = cq_ref.shape[0]
    scale = MLA_QK_DIM ** -0.5 * math.log2(math.e)
    cos = cos_ref[...]
    sin = sin_ref[...]
    lane = lax.broadcasted_iota(jnp.int32, (tm, LANES), 1)
    first_half = (lane % MLA_ROPE_DIM) < (MLA_ROPE_DIM // 2)
    low_head = lane < MLA_ROPE_DIM

    def rope(x):
        rotated = jnp.where(first_half,
                            -pltpu.roll(x, LANES - MLA_ROPE_DIM // 2, axis=1),
                            pltpu.roll(x, MLA_ROPE_DIM // 2, axis=1))
        return x * cos + rotated * sin

    cqn = _rms_norm(cq_ref[...].astype(F32), qg_ref[...]).astype(BF16)
    qn = jnp.dot(cqn, wqn_ref[...], preferred_element_type=F32) * scale
    qr = jnp.dot(cqn, wqr_ref[...], preferred_element_type=F32)
    for pair in range(MLA_HEADS // 2):
        r = rope(qr[:, pair * LANES:(pair + 1) * LANES]) * scale
        for sub in range(2):
            h = 2 * pair + sub
            q_ref[0, h, :, :MLA_NOPE_DIM] = qn[:, h * MLA_NOPE_DIM:(h + 1) * MLA_NOPE_DIM].astype(q_ref.dtype)
            keep = low_head if sub == 0 else jnp.logical_not(low_head)
            q_ref[0, h, :, MLA_NOPE_DIM:] = jnp.where(keep, r, 0.0).astype(q_ref.dtype)

    ckvn = _rms_norm(ckv_ref[...].astype(F32), kvg_ref[...]).astype(BF16)
    kn = jnp.dot(ckvn, wk_ref[...], preferred_element_type=F32)
    vt = lax.dot_general(wvt_ref[...], ckvn, (((1,), (1,)), ((), ())), preferred_element_type=F32)
    kr = rope(kr_ref[...].astype(F32))
    kr2 = jnp.where(low_head, kr, pltpu.roll(kr, MLA_ROPE_DIM, axis=1)).astype(k_ref.dtype)
    for h in range(MLA_HEADS):
        k_ref[0, h, :, :MLA_NOPE_DIM] = kn[:, h * MLA_NOPE_DIM:(h + 1) * MLA_NOPE_DIM].astype(k_ref.dtype)
        k_ref[0, h, :, MLA_NOPE_DIM:] = kr2
        vt_ref[0, h] = vt[h * MLA_V_DIM:(h + 1) * MLA_V_DIM, :].astype(vt_ref.dtype)


def _mla_proj(lat3d, cos, sin, q_norm_g, wq_nope, wq_rope, kv_norm_g, wk_nope, wv_t, l):
    b, s, _ = lat3d.shape
    nb = s // TM_MLA
    stacks = (q_norm_g, wq_nope, wq_rope, kv_norm_g, wk_nope, wv_t)
    tab = pl.BlockSpec((TM_MLA, LANES), lambda i, j: (i * nb + j, 0))

    def seg(width, col):
        return pl.BlockSpec((None, TM_MLA, width), lambda i, j: (i, j, col // width))

    def heads(width):
        return pl.BlockSpec((1, MLA_HEADS, TM_MLA, width), lambda i, j: (i, 0, j, 0))

    return pl.pallas_call(
        _mla_proj_kernel,
        grid=(b, nb),
        in_specs=[seg(MLA_Q_RANK, LAT_CQ), seg(MLA_KV_RANK, LAT_CKV), seg(LANES, LAT_KR), tab, tab]
                 + [_layer(p, l) for p in stacks],
        out_specs=[heads(QK_PAD), heads(QK_PAD),
                   pl.BlockSpec((1, MLA_HEADS, MLA_V_DIM, TM_MLA), lambda i, j: (i, 0, 0, j))],
        out_shape=[jax.ShapeDtypeStruct((b, MLA_HEADS, s, QK_PAD), BF16),
                   jax.ShapeDtypeStruct((b, MLA_HEADS, s, QK_PAD), BF16),
                   jax.ShapeDtypeStruct((b, MLA_HEADS, MLA_V_DIM, s), BF16)],
        compiler_params=_params(),
        name="mla_proj",
    )(lat3d, lat3d, lat3d, cos, sin, *stacks)


def _attn_kernel(q_ref, k_ref, vt_ref, o_ref, s0_ref, s1_ref):
    qi = pl.program_id(1)
    n_heads = q_ref.shape[1]
    hk = TK // 2
    nt_dims = (((1,), (1,)), ((), ()))
    key = lax.broadcasted_iota(jnp.int32, (hk, TQ), 0)
    qry = lax.broadcasted_iota(jnp.int32, (hk, TQ), 1)
    causal = key <= qry

    def col_max(s):
        return jnp.max(s, axis=0, keepdims=True)

    def produce(h, s_ref, nk):
        s = lax.dot_general(k_ref[0, h, :nk - hk, :], q_ref[0, h], nt_dims,
                            preferred_element_type=F32)
        s_low = lax.dot_general(k_ref[0, h, nk - hk:nk, :], q_ref[0, h, hk:, :], nt_dims,
                                preferred_element_type=F32)
        top = jnp.where(causal, s[nk - TK:], -jnp.inf)
        low = jnp.where(causal[:, :hk], s_low, -jnp.inf)
        s_ref[nk - TK:nk - hk] = top
        s_ref[nk - hk:nk, hk:] = low
        m = col_max(top)
        if nk > TK:
            s_ref[:nk - TK] = s[:nk - TK]
            m = jnp.maximum(m, col_max(s[:nk - TK]))
        return jnp.concatenate([m[:, :hk], jnp.maximum(m[:, hk:], col_max(low))], axis=1)

    def consume(h, s_ref, m, nk):
        p = jnp.exp2(s_ref[:nk - hk] - m)
        p_low = jnp.exp2(s_ref[nk - hk:nk, hk:] - m[:, hk:])
        l = jnp.sum(p, axis=0, keepdims=True)
        l = jnp.concatenate([l[:, :hk], l[:, hk:] + jnp.sum(p_low, axis=0, keepdims=True)], axis=1)
        pb = p.astype(BF16)
        o_first = jnp.dot(vt_ref[0, h, :, :nk - hk], pb[:, :hk], preferred_element_type=F32)
        o_last = jnp.dot(vt_ref[0, h, :, :nk], jnp.concatenate([pb[:, hk:], p_low.astype(BF16)], axis=0),
                         preferred_element_type=F32)
        o = jnp.concatenate([o_first, o_last], axis=1) / l
        o_ref[0, h] = o.T.astype(o_ref.dtype)

    def run(nk):
        bufs = (s0_ref, s1_ref)
        m = produce(0, bufs[0], nk)
        for h in range(n_heads):
            m_next = produce(h + 1, bufs[(h + 1) % 2], nk) if h + 1 < n_heads else None
            consume(h, bufs[h % 2], m, nk)
            m = m_next

    for c in range(k_ref.shape[2] // TK):
        pl.when(qi == c)(functools.partial(run, (c + 1) * TK))


def _attention(q, k, vt):
    b, h, s, _ = q.shape
    assert TQ == TK and h % 2 == 0
    return pl.pallas_call(
        _attn_kernel,
        grid=(b, s // TQ),
        in_specs=[pl.BlockSpec((1, h, TQ, QK_PAD), lambda i, j: (i, 0, j, 0)),
                  pl.BlockSpec((1, h, s, QK_PAD), lambda i, j: (i, 0, 0, 0)),
                  pl.BlockSpec((1, h, MLA_V_DIM, s), lambda i, j: (i, 0, 0, 0))],
        out_specs=pl.BlockSpec((1, h, TQ, MLA_V_DIM), lambda i, j: (i, 0, j, 0)),
        out_shape=jax.ShapeDtypeStruct((b, h, s, MLA_V_DIM), BF16),
        scratch_shapes=[pltpu.VMEM((s, TQ), F32), pltpu.VMEM((s, TQ), F32)],
        compiler_params=_params(),
        name="mla_attention",
    )(q, k, vt)


def _out_proj_kernel(ya_ref, yb_ref, bg_ref, yc_ref, h_ref, ing_ref, inb_ref, w_ref, bo_ref,
                     pg_ref, pb_ref, o_ref, *, residual_ln):
    attn = jnp.concatenate([yb_ref[0, h] for h in range(MLA_HEADS)], axis=1).astype(F32)
    yb = (attn * _silu(bg_ref[...].astype(F32))).astype(BF16)
    y = jnp.concatenate([ya_ref[...], yb, yc_ref[...]], axis=1)
    out = jnp.dot(y, w_ref[...], preferred_element_type=F32) + bo_ref[...]
    h = h_ref[...]
    if residual_ln:
        h = _layer_norm(h, ing_ref[...], inb_ref[...])
    o_ref[...] = _layer_norm(ALPHA * h + out, pg_ref[...], pb_ref[...])


def _out_proj(y_a, y_b, b_gate, y_c, h3d, ln_in_g, ln_in_b, w_out, b_out, post_g, post_b, l, *, residual_ln):
    b, s, _ = h3d.shape

    def rows(width):
        return pl.BlockSpec((None, TM_OUT, width), lambda i, j: (i, j, 0))

    vec = _resident((1, D_MODEL))
    return pl.pallas_call(
        functools.partial(_out_proj_kernel, residual_ln=residual_ln),
        grid=(b, s // TM_OUT),
        in_specs=[rows(POOL_WIDTH),
                  pl.BlockSpec((1, MLA_HEADS, TM_OUT, MLA_V_DIM), lambda i, j: (i, 0, j, 0)),
                  rows(MLA_WIDTH), rows(SGU_WIDTH), rows(D_MODEL),
                  vec, vec, _layer(w_out, l), _layer(b_out, l), _layer(post_g, l), _layer(post_b, l)],
        out_specs=rows(D_MODEL),
        out_shape=jax.ShapeDtypeStruct((b, s, D_MODEL), F32),
        compiler_params=_params(),
        name="out_proj",
    )(y_a, y_b, b_gate, y_c, h3d, ln_in_g, ln_in_b, w_out, b_out, post_g, post_b)


def kernel(x, positions, ln_in_g, ln_in_b, w_in, pool_w, pool_scale, q_norm_g, w_uq, kv_norm_g, w_ukv,
           sgu_norm_g, sgu_norm_b, sgu_w, sgu_b, w_out, b_out, ln_post_g, ln_post_b):
    b, s, d = x.shape
    assert d == D_MODEL and s % TQ == 0 and s % TM_OUT == 0 and s % TM_FRONT == 0 and s % TM_MLA == 0

    cos, sin = _rope_tables(positions)
    ln_in_g2, ln_in_b2 = ln_in_g[None, :], ln_in_b[None, :]

    depth = w_in.shape[0]
    assert depth == DEPTH
    w_lat, w_mix, w_bg = _regroup_in_proj(w_in)
    wq = w_uq.reshape(depth, MLA_Q_RANK, MLA_HEADS, MLA_QK_DIM)
    wq_nope = wq[..., :MLA_NOPE_DIM].reshape(depth, MLA_Q_RANK, MLA_HEADS * MLA_NOPE_DIM).astype(BF16)
    wq_rope = wq[..., MLA_NOPE_DIM:].reshape(depth, MLA_Q_RANK, MLA_HEADS * MLA_ROPE_DIM).astype(BF16)
    wkv = w_ukv.reshape(depth, MLA_KV_RANK, MLA_HEADS, MLA_NOPE_DIM + MLA_V_DIM)
    wk_nope = wkv[..., :MLA_NOPE_DIM].reshape(depth, MLA_KV_RANK, MLA_HEADS * MLA_NOPE_DIM).astype(BF16)
    wv_t = wkv[..., MLA_NOPE_DIM:].reshape(depth, MLA_KV_RANK, MLA_HEADS * MLA_V_DIM).swapaxes(1, 2).astype(BF16)
    w_pool = pool_w.astype(BF16)
    w_out_b = w_out.astype(BF16)

    def row(p):
        return p[:, None, :]

    h = x
    for l in range(depth):
        first = l == 0
        lat, b_gate, y_a, y_c = _front(
            h, ln_in_g2, ln_in_b2, w_lat, w_mix, w_bg, w_pool, row(pool_scale),
            row(sgu_norm_g), row(sgu_norm_b), sgu_w, sgu_b[..., None], l, apply_ln=first)
        q, k, vt = _mla_proj(lat, cos, sin, row(q_norm_g), wq_nope, wq_rope, row(kv_norm_g), wk_nope, wv_t, l)
        y_b = _attention(q, k, vt)
        h = _out_proj(y_a, y_b, b_gate, y_c, h, ln_in_g2, ln_in_b2,
                      w_out_b, row(b_out), row(ln_post_g), row(ln_post_b), l, residual_ln=first)
    return h
```

```python
import functools
import math

import jax
import jax.numpy as jnp
from jax import lax
from jax.experimental import pallas as pl
from jax.experimental.pallas import tpu as pltpu

F32 = jnp.float32
BF16 = jnp.bfloat16

D_MODEL = 2048
DEPTH = 2
EPS = 1e-5
POOL_WIDTH = 512
POOL_WINDOWS = (2, 4, 8, 16)
POOL_GROUP_DIM = 128
MLA_HEADS = 8
MLA_NOPE_DIM = 128
MLA_ROPE_DIM = 64
MLA_V_DIM = 128
MLA_WIDTH = MLA_HEADS * MLA_V_DIM
MLA_Q_RANK = 512
MLA_KV_RANK = 256
MLA_QK_DIM = MLA_NOPE_DIM + MLA_ROPE_DIM
ROPE_THETA = 10000.0
SGU_WIDTH = 512
SGU_HEADS = 4
SGU_HEAD_DIM = 128
SGU_CHUNK = 128
ALPHA = (2.0 * DEPTH) ** 0.25

LANES = 128
MXU_DIM = 256
VMEM_LIMIT = 56 * 1024 * 1024

SRC_AX = 0
SRC_AG = SRC_AX + POOL_WIDTH
SRC_CQ = SRC_AG + POOL_WIDTH
SRC_CKV = SRC_CQ + MLA_Q_RANK
SRC_KR = SRC_CKV + MLA_KV_RANK
SRC_BG = SRC_KR + MLA_ROPE_DIM
SRC_UV = SRC_BG + MLA_WIDTH
SRC_CG = SRC_UV + 2 * SGU_WIDTH
SRC_END = SRC_CG + SGU_WIDTH
LAT_CQ = 0
LAT_CKV = LAT_CQ + MLA_Q_RANK
LAT_KR = LAT_CKV + MLA_KV_RANK
LAT_COLS = LAT_KR + MXU_DIM
QK_PAD = MXU_DIM
POOL_HALO = 16

TM_FRONT = 512
TM_MLA = 512
TQ = 512
TK = 512
TM_OUT = 512


def _layer_norm(x, g, b):
    mu = jnp.mean(x, axis=-1, keepdims=True)
    xc = x - mu
    var = jnp.mean(xc * xc, axis=-1, keepdims=True)
    return xc * lax.rsqrt(var + EPS) * g + b


def _rms_norm(x, g):
    ms = jnp.mean(x * x, axis=-1, keepdims=True)
    return x * lax.rsqrt(ms + EPS) * g


def _silu(x):
    return x / (1.0 + jnp.exp(-x))


def _resident(shape):
    return pl.BlockSpec(shape, lambda *_: (0,) * len(shape), pipeline_mode=pl.Buffered(1))


def _layer(stacked, l):
    tail = stacked.shape[1:]
    return pl.BlockSpec((None,) + tail, lambda *_: (l,) + (0,) * len(tail), pipeline_mode=pl.Buffered(1))


def _params():
    return pltpu.CompilerParams(vmem_limit_bytes=VMEM_LIMIT)


def _rope_table_kernel(pos_ref, freq_ref, cos_ref, sin_ref):
    half = MLA_ROPE_DIM // 2
    ang = pos_ref[...].astype(F32) * freq_ref[...]
    lane_group = lax.broadcasted_iota(jnp.int32, ang.shape, 1) // half

    def spread(x, r):
        only_r = jnp.where(lane_group == r, x, 0.0)
        return sum((pltpu.roll(only_r, k * half, axis=1) for k in range(1, LANES // half)), only_r)

    cos, sin = jnp.cos(ang), jnp.sin(ang)
    for r in range(LANES // half):
        cos_ref[r] = spread(cos, r)
        sin_ref[r] = spread(sin, r)


def _rope_tables(positions):
    t = positions.size
    half = MLA_ROPE_DIM // 2
    per_row = LANES // half
    rows = t // per_row
    inv_freq = ROPE_THETA ** (-jnp.arange(half, dtype=F32) / half)
    freq_row = jnp.tile(inv_freq, per_row)[None, :]
    pos_rows = jnp.repeat(positions.reshape(per_row, rows).T, half, axis=1)
    tm = 1024
    cos, sin = pl.pallas_call(
        _rope_table_kernel,
        grid=(rows // tm,),
        in_specs=[pl.BlockSpec((tm, LANES), lambda i: (i, 0)), _resident((1, LANES))],
        out_specs=[pl.BlockSpec((per_row, tm, LANES), lambda i: (0, i, 0))] * 2,
        out_shape=[jax.ShapeDtypeStruct((per_row, rows, LANES), F32)] * 2,
        compiler_params=_params(),
        name="rope_tables",
    )(pos_rows, freq_row)
    return cos.reshape(t, LANES), sin.reshape(t, LANES)


def _pool_mixer(ax, ag, seq_tile, halo_ref, w_ref, sc_ref, o_ref):
    tm = ax.shape[0]

    @pl.when(seq_tile == 0)
    def _():
        halo_ref[...] = jnp.zeros_like(halo_ref)

    halo = halo_ref[...]
    t = seq_tile * tm + lax.broadcasted_iota(jnp.int32, (tm, POOL_GROUP_DIM), 0)
    for gi, win in enumerate(POOL_WINDOWS):
        cols = slice(gi * POOL_GROUP_DIM, (gi + 1) * POOL_GROUP_DIM)
        x = ax[:, cols]
        acc = jnp.concatenate([halo[:, cols], x], axis=0)
        span = 1
        while span < win:
            acc = acc + pltpu.roll(acc, span, axis=0)
            span *= 2
        count = jnp.minimum(t + 1, win).astype(F32)
        d = (acc[POOL_HALO:] / count - x).astype(BF16)
        y = jnp.dot(d, w_ref[gi], preferred_element_type=F32)
        o_ref[:, cols] = (y * sc_ref[:, cols] * _silu(ag[:, cols])).astype(o_ref.dtype)
    halo_ref[...] = ax[tm - POOL_HALO:, :]


def _sgu_mixer(uv, cg, ng_ref, nb_ref, ws_ref, bs_ref, o_ref):
    tm = uv.shape[0]
    uv = 0.5 * uv * (1.0 + lax.erf(uv * (2.0 ** -0.5)))
    u = uv[:, :SGU_WIDTH]
    vb = _layer_norm(uv[:, SGU_WIDTH:], ng_ref[...], nb_ref[...]).astype(BF16)
    gate = _silu(cg)
    n_chunks = tm // SGU_CHUNK
    row = lax.broadcasted_iota(jnp.int32, (SGU_CHUNK, SGU_CHUNK), 0)
    col = lax.broadcasted_iota(jnp.int32, (SGU_CHUNK, SGU_CHUNK), 1)
    for h in range(SGU_HEADS):
        cols = slice(h * SGU_HEAD_DIM, (h + 1) * SGU_HEAD_DIM)
        w = jnp.where(col <= row, ws_ref[h], 0.0).astype(BF16)
        vh = jnp.concatenate([vb[c * SGU_CHUNK:(c + 1) * SGU_CHUNK, cols] for c in range(n_chunks)], axis=1)
        mixed = jnp.dot(w, vh, preferred_element_type=F32) + bs_ref[h]
        for c in range(n_chunks):
            rows = slice(c * SGU_CHUNK, (c + 1) * SGU_CHUNK)
            piece = mixed[:, c * SGU_HEAD_DIM:(c + 1) * SGU_HEAD_DIM]
            o_ref[rows, cols] = (u[rows, cols] * piece * gate[rows, cols]).astype(o_ref.dtype)


def _front_kernel(x_ref, g_ref, b_ref, wt_ref, pw_ref, psc_ref, ng_ref, nb_ref,
                  ws_ref, bs_ref, lat_ref, bg_ref, ya_ref, yc_ref, halo_ref, *, apply_ln):
    x = x_ref[...]
    if apply_ln:
        x = _layer_norm(x, g_ref[...], b_ref[...])
    xb = x.astype(BF16)

    def proj(lo, hi):
        return lax.dot_general(xb, wt_ref[lo:hi, :], (((1,), (1,)), ((), ())), preferred_element_type=F32)

    uv, cg = proj(SRC_UV, SRC_CG), proj(SRC_CG, SRC_END)
    ax, ag = proj(SRC_AX, SRC_AG), proj(SRC_AG, SRC_CQ)
    n_lat = SRC_BG - SRC_CQ
    lat_ref[:, :n_lat] = proj(SRC_CQ, SRC_BG).astype(lat_ref.dtype)
    lat_ref[:, n_lat:] = jnp.zeros((xb.shape[0], LAT_COLS - n_lat), lat_ref.dtype)
    _sgu_mixer(uv, cg, ng_ref, nb_ref, ws_ref, bs_ref, yc_ref)
    _pool_mixer(ax, ag, pl.program_id(1), halo_ref, pw_ref, psc_ref, ya_ref)
    bg_ref[...] = proj(SRC_BG, SRC_UV).astype(bg_ref.dtype)


def _front(x3d, ln_g, ln_b, w_in_t, w_pool, pool_scale, sgu_g, sgu_b, w_s, b_s_col, l, *, apply_ln):
    b, s, _ = x3d.shape
    assert w_in_t.shape[1:] == (SRC_END, D_MODEL)
    assert TM_FRONT % SGU_CHUNK == 0 and POOL_HALO >= max(POOL_WINDOWS) - 1
    stacks = (w_in_t, w_pool, pool_scale, sgu_g, sgu_b, w_s, b_s_col)

    def rows(width):
        return pl.BlockSpec((None, TM_FRONT, width), lambda i, j: (i, j, 0))

    def out(width):
        return jax.ShapeDtypeStruct((b, s, width), BF16)

    return pl.pallas_call(
        functools.partial(_front_kernel, apply_ln=apply_ln),
        grid=(b, s // TM_FRONT),
        in_specs=[rows(D_MODEL), _resident((1, D_MODEL)), _resident((1, D_MODEL))]
                 + [_layer(p, l) for p in stacks],
        out_specs=[rows(LAT_COLS), rows(MLA_WIDTH), rows(POOL_WIDTH), rows(SGU_WIDTH)],
        out_shape=[out(LAT_COLS), out(MLA_WIDTH), out(POOL_WIDTH), out(SGU_WIDTH)],
        scratch_shapes=[pltpu.VMEM((POOL_HALO, POOL_WIDTH), F32)],
        compiler_params=pltpu.CompilerParams(vmem_limit_bytes=VMEM_LIMIT,
                                             dimension_semantics=("arbitrary", "arbitrary")),
        name="front",
    )(x3d, ln_g, ln_b, *stacks)


def _mla_proj_kernel(cq_ref, ckv_ref, kr_ref, cos_ref, sin_ref, qg_ref, wqn_ref, wqr_ref,
                     kvg_ref, wk_ref, wvt_ref, q_ref, k_ref, vt_ref):
    tm = cq_ref.shape[0]
    scale = MLA_QK_DIM ** -0.5 * math.log2(math.e)
    cos = cos_ref[...]
    sin = sin_ref[...]
    lane = lax.broadcasted_iota(jnp.int32, (tm, LANES), 1)
    first_half = (lane % MLA_ROPE_DIM) < (MLA_ROPE_DIM // 2)
    low_head = lane < MLA_ROPE_DIM

    def rope(x):
        rotated = jnp.where(first_half,
                            -pltpu.roll(x, LANES - MLA_ROPE_DIM // 2, axis=1),
                            pltpu.roll(x, MLA_ROPE_DIM // 2, axis=1))
        return x * cos + rotated * sin

    cqn = _rms_norm(cq_ref[...].astype(F32), qg_ref[...]).astype(BF16)
    qn = jnp.dot(cqn, wqn_ref[...], preferred_element_type=F32) * scale
    qr = jnp.dot(cqn, wqr_ref[...], preferred_element_type=F32)
    for pair in range(MLA_HEADS // 2):
        r = rope(qr[:, pair * LANES:(pair + 1) * LANES]) * scale
        for sub in range(2):
            h = 2 * pair + sub
            q_ref[0, h, :, :MLA_NOPE_DIM] = qn[:, h * MLA_NOPE_DIM:(h + 1) * MLA_NOPE_DIM].astype(q_ref.dtype)
            keep = low_head if sub == 0 else jnp.logical_not(low_head)
            q_ref[0, h, :, MLA_NOPE_DIM:] = jnp.where(keep, r, 0.0).astype(q_ref.dtype)

    ckvn = _rms_norm(ckv_ref[...].astype(F32), kvg_ref[...]).astype(BF16)
    kn = jnp.dot(ckvn, wk_ref[...], preferred_element_type=F32)
    vt = lax.dot_general(wvt_ref[...], ckvn, (((1,), (1,)), ((), ())), preferred_element_type=F32)
    kr = rope(kr_ref[...].astype(F32))
    kr2 = jnp.where(low_head, kr, pltpu.roll(kr, MLA_ROPE_DIM, axis=1)).astype(k_ref.dtype)
    for h in range(MLA_HEADS):
        k_ref[0, h, :, :MLA_NOPE_DIM] = kn[:, h * MLA_NOPE_DIM:(h + 1) * MLA_NOPE_DIM].astype(k_ref.dtype)
        k_ref[0, h, :, MLA_NOPE_DIM:] = kr2
        vt_ref[0, h] = vt[h * MLA_V_DIM:(h + 1) * MLA_V_DIM, :].astype(vt_ref.dtype)


def _mla_proj(lat3d, cos, sin, q_norm_g, wq_nope, wq_rope, kv_norm_g, wk_nope, wv_t, l):
    b, s, _ = lat3d.shape
    nb = s // TM_MLA
    stacks = (q_norm_g, wq_nope, wq_rope, kv_norm_g, wk_nope, wv_t)
    tab = pl.BlockSpec((TM_MLA, LANES), lambda i, j: (i * nb + j, 0))

    def seg(width, col):
        return pl.BlockSpec((None, TM_MLA, width), lambda i, j: (i, j, col // width))

    def heads(width):
        return pl.BlockSpec((1, MLA_HEADS, TM_MLA, width), lambda i, j: (i, 0, j, 0))

    return pl.pallas_call(
        _mla_proj_kernel,
        grid=(b, nb),
        in_specs=[seg(MLA_Q_RANK, LAT_CQ), seg(MLA_KV_RANK, LAT_CKV), seg(LANES, LAT_KR), tab, tab]
                 + [_layer(p, l) for p in stacks],
        out_specs=[heads(QK_PAD), heads(QK_PAD),
                   pl.BlockSpec((1, MLA_HEADS, MLA_V_DIM, TM_MLA), lambda i, j: (i, 0, 0, j))],
        out_shape=[jax.ShapeDtypeStruct((b, MLA_HEADS, s, QK_PAD), BF16),
                   jax.ShapeDtypeStruct((b, MLA_HEADS, s, QK_PAD), BF16),
                   jax.ShapeDtypeStruct((b, MLA_HEADS, MLA_V_DIM, s), BF16)],
        compiler_params=_params(),
        name="mla_proj",
    )(lat3d, lat3d, lat3d, cos, sin, *stacks)


def _attn_kernel(q_ref, k_ref, vt_ref, o_ref, s0_ref, s1_ref):
    qi = pl.program_id(1)
    n_heads = q_ref.shape[1]
    hk = TK // 2
    nt_dims = (((1,), (1,)), ((), ()))
    key = lax.broadcasted_iota(jnp.int32, (hk, TQ), 0)
    qry = lax.broadcasted_iota(jnp.int32, (hk, TQ), 1)
    causal = key <= qry

    def col_max(s):
        return jnp.max(s, axis=0, keepdims=True)

    def produce(h, s_ref, nk):
        s = lax.dot_general(k_ref[0, h, :nk - hk, :], q_ref[0, h], nt_dims,
                            preferred_element_type=F32)
        s_low = lax.dot_general(k_ref[0, h, nk - hk:nk, :], q_ref[0, h, hk:, :], nt_dims,
                                preferred_element_type=F32)
        top = jnp.where(causal, s[nk - TK:], -jnp.inf)
        low = jnp.where(causal[:, :hk], s_low, -jnp.inf)
        s_ref[nk - TK:nk - hk] = top
        s_ref[nk - hk:nk, hk:] = low
        m = col_max(top)
        if nk > TK:
            s_ref[:nk - TK] = s[:nk - TK]
            m = jnp.maximum(m, col_max(s[:nk - TK]))
        return jnp.concatenate([m[:, :hk], jnp.maximum(m[:, hk:], col_max(low))], axis=1)

    def consume(h, s_ref, m, nk):
        p = jnp.exp2(s_ref[:nk - hk] - m)
        p_low = jnp.exp2(s_ref[nk - hk:nk, hk:] - m[:, hk:])
        l = jnp.sum(p, axis=0, keepdims=True)
        l = jnp.concatenate([l[:, :hk], l[:, hk:] + jnp.sum(p_low, axis=0, keepdims=True)], axis=1)
        pb = p.astype(BF16)
        o_first = jnp.dot(vt_ref[0, h, :, :nk - hk], pb[:, :hk], preferred_element_type=F32)
        o_last = jnp.dot(vt_ref[0, h, :, :nk], jnp.concatenate([pb[:, hk:], p_low.astype(BF16)], axis=0),
                         preferred_element_type=F32)
        o = jnp.concatenate([o_first, o_last], axis=1) / l
        o_ref[0, h] = o.T.astype(o_ref.dtype)

    def run(nk):
        bufs = (s0_ref, s1_ref)
        m = produce(0, bufs[0], nk)
        for h in range(n_heads):
            m_next = produce(h + 1, bufs[(h + 1) % 2], nk) if h + 1 < n_heads else None
            consume(h, bufs[h % 2], m, nk)
            m = m_next

    for c in range(k_ref.shape[2] // TK):
        pl.when(qi == c)(functools.partial(run, (c + 1) * TK))


def _attention(q, k, vt):
    b, h, s, _ = q.shape
    assert TQ == TK and h % 2 == 0
    return pl.pallas_call(
        _attn_kernel,
        grid=(b, s // TQ),
        in_specs=[pl.BlockSpec((1, h, TQ, QK_PAD), lambda i, j: (i, 0, j, 0)),
                  pl.BlockSpec((1, h, s, QK_PAD), lambda i, j: (i, 0, 0, 0)),
                  pl.BlockSpec((1, h, MLA_V_DIM, s), lambda i, j: (i, 0, 0, 0))],
        out_specs=pl.BlockSpec((1, h, TQ, MLA_V_DIM), lambda i, j: (i, 0, j, 0)),
        out_shape=jax.ShapeDtypeStruct((b, h, s, MLA_V_DIM), BF16),
        scratch_shapes=[pltpu.VMEM((s, TQ), F32), pltpu.VMEM((s, TQ), F32)],
        compiler_params=_params(),
        name="mla_attention",
    )(q, k, vt)


def _out_proj_kernel(ya_ref, yb_ref, bg_ref, yc_ref, h_ref, ing_ref, inb_ref, w_ref, bo_ref,
                     pg_ref, pb_ref, o_ref, *, residual_ln):
    attn = jnp.concatenate([yb_ref[0, h] for h in range(MLA_HEADS)], axis=1).astype(F32)
    yb = (attn * _silu(bg_ref[...].astype(F32))).astype(BF16)
    y = jnp.concatenate([ya_ref[...], yc_ref[...], yb], axis=1)
    out = jnp.dot(y, w_ref[...], preferred_element_type=F32) + bo_ref[...]
    h = h_ref[...]
    if residual_ln:
        h = _layer_norm(h, ing_ref[...], inb_ref[...])
    o_ref[...] = _layer_norm(ALPHA * h + out, pg_ref[...], pb_ref[...])


def _out_proj(y_a, y_b, b_gate, y_c, h3d, ln_in_g, ln_in_b, w_out, b_out, post_g, post_b, l, *, residual_ln):
    b, s, _ = h3d.shape

    def rows(width):
        return pl.BlockSpec((None, TM_OUT, width), lambda i, j: (i, j, 0))

    vec = _resident((1, D_MODEL))
    return pl.pallas_call(
        functools.partial(_out_proj_kernel, residual_ln=residual_ln),
        grid=(b, s // TM_OUT),
        in_specs=[rows(POOL_WIDTH),
                  pl.BlockSpec((1, MLA_HEADS, TM_OUT, MLA_V_DIM), lambda i, j: (i, 0, j, 0)),
                  rows(MLA_WIDTH), rows(SGU_WIDTH), rows(D_MODEL),
                  vec, vec, _layer(w_out, l), _layer(b_out, l), _layer(post_g, l), _layer(post_b, l)],
        out_specs=rows(D_MODEL),
        out_shape=jax.ShapeDtypeStruct((b, s, D_MODEL), F32),
        compiler_params=_params(),
        name="out_proj",
    )(y_a, y_b, b_gate, y_c, h3d, ln_in_g, ln_in_b, w_out, b_out, post_g, post_b)


def kernel(x, positions, ln_in_g, ln_in_b, w_in, pool_w, pool_scale, q_norm_g, w_uq, kv_norm_g, w_ukv,
           sgu_norm_g, sgu_norm_b, sgu_w, sgu_b, w_out, b_out, ln_post_g, ln_post_b):
    b, s, d = x.shape
    assert d == D_MODEL and s % TQ == 0 and s % TM_OUT == 0 and s % TM_FRONT == 0 and s % TM_MLA == 0

    cos, sin = _rope_tables(positions)
    ln_in_g2, ln_in_b2 = ln_in_g[None, :], ln_in_b[None, :]

    depth = w_in.shape[0]
    assert depth == DEPTH
    w_in_t = jnp.swapaxes(w_in, 1, 2).astype(BF16)
    wq = w_uq.reshape(depth, MLA_Q_RANK, MLA_HEADS, MLA_QK_DIM)
    wq_nope = wq[..., :MLA_NOPE_DIM].reshape(depth, MLA_Q_RANK, MLA_HEADS * MLA_NOPE_DIM).astype(BF16)
    wq_rope = wq[..., MLA_NOPE_DIM:].reshape(depth, MLA_Q_RANK, MLA_HEADS * MLA_ROPE_DIM).astype(BF16)
    wkv = w_ukv.reshape(depth, MLA_KV_RANK, MLA_HEADS, MLA_NOPE_DIM + MLA_V_DIM)
    wk_nope = wkv[..., :MLA_NOPE_DIM].reshape(depth, MLA_KV_RANK, MLA_HEADS * MLA_NOPE_DIM).astype(BF16)
    wv_t = wkv[..., MLA_NOPE_DIM:].reshape(depth, MLA_KV_RANK, MLA_HEADS * MLA_V_DIM).swapaxes(1, 2).astype(BF16)
    w_pool = pool_w.astype(BF16)
    attn_lo, attn_hi = POOL_WIDTH, POOL_WIDTH + MLA_WIDTH
    w_out_b = jnp.concatenate([w_out[:, :attn_lo], w_out[:, attn_hi:], w_out[:, attn_lo:attn_hi]],
                              axis=1).astype(BF16)

    def row(p):
        return p[:, None, :]

    h = x
    for l in range(depth):
        first = l == 0
        lat, b_gate, y_a, y_c = _front(
            h, ln_in_g2, ln_in_b2, w_in_t, w_pool, row(pool_scale),
            row(sgu_norm_g), row(sgu_norm_b), sgu_w, sgu_b[..., None], l, apply_ln=first)
        q, k, vt = _mla_proj(lat, cos, sin, row(q_norm_g), wq_nope, wq_rope, row(kv_norm_g), wk_nope, wv_t, l)
        y_b = _attention(q, k, vt)
        h = _out_proj(y_a, y_b, b_gate, y_c, h, ln_in_g2, ln_in_b2,
                      w_out_b, row(b_out), row(ln_post_g), row(ln_post_b), l, residual_ln=first)
    return h
```

```python
import functools
import math

import jax
import jax.numpy as jnp
from jax import lax
from jax.experimental import pallas as pl
from jax.experimental.pallas import tpu as pltpu

F32 = jnp.float32
BF16 = jnp.bfloat16

D_MODEL = 2048
DEPTH = 2
EPS = 1e-5
POOL_WIDTH = 512
POOL_WINDOWS = (2, 4, 8, 16)
POOL_GROUP_DIM = 128
MLA_HEADS = 8
MLA_NOPE_DIM = 128
MLA_ROPE_DIM = 64
MLA_V_DIM = 128
MLA_WIDTH = MLA_HEADS * MLA_V_DIM
MLA_Q_RANK = 512
MLA_KV_RANK = 256
MLA_QK_DIM = MLA_NOPE_DIM + MLA_ROPE_DIM
ROPE_THETA = 10000.0
SGU_WIDTH = 512
SGU_HEADS = 4
SGU_HEAD_DIM = 128
SGU_CHUNK = 128
ALPHA = (2.0 * DEPTH) ** 0.25

LANES = 128
MXU_DIM = 256
VMEM_LIMIT = 56 * 1024 * 1024

SRC_AX = 0
SRC_AG = SRC_AX + POOL_WIDTH
SRC_CQ = SRC_AG + POOL_WIDTH
SRC_CKV = SRC_CQ + MLA_Q_RANK
SRC_KR = SRC_CKV + MLA_KV_RANK
SRC_BG = SRC_KR + MLA_ROPE_DIM
SRC_UV = SRC_BG + MLA_WIDTH
SRC_CG = SRC_UV + 2 * SGU_WIDTH
SRC_END = SRC_CG + SGU_WIDTH
LAT_CQ = 0
LAT_CKV = LAT_CQ + MLA_Q_RANK
LAT_KR = LAT_CKV + MLA_KV_RANK
LAT_COLS = LAT_KR + MXU_DIM
POOL_HALO = 16

TM_FRONT = 512
TM_MLA = 512
TQ = 512
TK = 512
TM_OUT = 512


def _layer_norm(x, g, b):
    mu = jnp.mean(x, axis=-1, keepdims=True)
    xc = x - mu
    var = jnp.mean(xc * xc, axis=-1, keepdims=True)
    return xc * lax.rsqrt(var + EPS) * g + b


def _rms_norm(x, g):
    ms = jnp.mean(x * x, axis=-1, keepdims=True)
    return x * lax.rsqrt(ms + EPS) * g


def _silu(x):
    return x / (1.0 + jnp.exp(-x))


def _resident(shape):
    return pl.BlockSpec(shape, lambda *_: (0,) * len(shape), pipeline_mode=pl.Buffered(1))


def _layer(stacked, l):
    tail = stacked.shape[1:]
    return pl.BlockSpec((None,) + tail, lambda *_: (l,) + (0,) * len(tail), pipeline_mode=pl.Buffered(1))


def _params():
    return pltpu.CompilerParams(vmem_limit_bytes=VMEM_LIMIT)


def _rope_table_kernel(pos_ref, freq_ref, cos_ref, sin_ref):
    half = MLA_ROPE_DIM // 2
    ang = pos_ref[...].astype(F32) * freq_ref[...]
    lane_group = lax.broadcasted_iota(jnp.int32, ang.shape, 1) // half

    def spread(x, r):
        only_r = jnp.where(lane_group == r, x, 0.0)
        return sum((pltpu.roll(only_r, k * half, axis=1) for k in range(1, LANES // half)), only_r)

    cos, sin = jnp.cos(ang), jnp.sin(ang)
    for r in range(LANES // half):
        cos_ref[r] = spread(cos, r)
        sin_ref[r] = spread(sin, r)


def _rope_tables(positions):
    t = positions.size
    half = MLA_ROPE_DIM // 2
    per_row = LANES // half
    rows = t // per_row
    inv_freq = ROPE_THETA ** (-jnp.arange(half, dtype=F32) / half)
    freq_row = jnp.tile(inv_freq, per_row)[None, :]
    pos_rows = jnp.repeat(positions.reshape(per_row, rows).T, half, axis=1)
    tm = 1024
    cos, sin = pl.pallas_call(
        _rope_table_kernel,
        grid=(rows // tm,),
        in_specs=[pl.BlockSpec((tm, LANES), lambda i: (i, 0)), _resident((1, LANES))],
        out_specs=[pl.BlockSpec((per_row, tm, LANES), lambda i: (0, i, 0))] * 2,
        out_shape=[jax.ShapeDtypeStruct((per_row, rows, LANES), F32)] * 2,
        compiler_params=_params(),
        name="rope_tables",
    )(pos_rows, freq_row)
    return cos.reshape(t, LANES), sin.reshape(t, LANES)


def _pool_mixer(ax, ag, seq_tile, halo_ref, w_ref, sc_ref, o_ref):
    tm = ax.shape[0]

    @pl.when(seq_tile == 0)
    def _():
        halo_ref[...] = jnp.zeros_like(halo_ref)

    halo = halo_ref[...]
    t = seq_tile * tm + lax.broadcasted_iota(jnp.int32, (tm, POOL_GROUP_DIM), 0)
    for gi, win in enumerate(POOL_WINDOWS):
        cols = slice(gi * POOL_GROUP_DIM, (gi + 1) * POOL_GROUP_DIM)
        x = ax[:, cols]
        acc = jnp.concatenate([halo[:, cols], x], axis=0)
        span = 1
        while span < win:
            acc = acc + pltpu.roll(acc, span, axis=0)
            span *= 2
        count = jnp.minimum(t + 1, win).astype(F32)
        d = (acc[POOL_HALO:] / count - x).astype(BF16)
        y = jnp.dot(d, w_ref[gi], preferred_element_type=F32)
        o_ref[:, cols] = (y * sc_ref[:, cols] * _silu(ag[:, cols])).astype(o_ref.dtype)
    halo_ref[...] = ax[tm - POOL_HALO:, :]


def _sgu_mixer(uv, cg, ng_ref, nb_ref, ws_ref, bs_ref, o_ref):
    tm = uv.shape[0]
    uv = 0.5 * uv * (1.0 + lax.erf(uv * (2.0 ** -0.5)))
    u = uv[:, :SGU_WIDTH]
    vb = _layer_norm(uv[:, SGU_WIDTH:], ng_ref[...], nb_ref[...]).astype(BF16)
    gate = _silu(cg)
    n_chunks = tm // SGU_CHUNK
    row = lax.broadcasted_iota(jnp.int32, (SGU_CHUNK, SGU_CHUNK), 0)
    col = lax.broadcasted_iota(jnp.int32, (SGU_CHUNK, SGU_CHUNK), 1)
    for h in range(SGU_HEADS):
        cols = slice(h * SGU_HEAD_DIM, (h + 1) * SGU_HEAD_DIM)
        w = jnp.where(col <= row, ws_ref[h], 0.0).astype(BF16)
        vh = jnp.concatenate([vb[c * SGU_CHUNK:(c + 1) * SGU_CHUNK, cols] for c in range(n_chunks)], axis=1)
        mixed = jnp.dot(w, vh, preferred_element_type=F32) + bs_ref[h]
        for c in range(n_chunks):
            rows = slice(c * SGU_CHUNK, (c + 1) * SGU_CHUNK)
            piece = mixed[:, c * SGU_HEAD_DIM:(c + 1) * SGU_HEAD_DIM]
            o_ref[rows, cols] = (u[rows, cols] * piece * gate[rows, cols]).astype(o_ref.dtype)


def _front_kernel(x_ref, g_ref, b_ref, wt_ref, pw_ref, psc_ref, ng_ref, nb_ref,
                  ws_ref, bs_ref, lat_ref, bg_ref, ya_ref, yc_ref, *rest, apply_ln):
    x = x_ref[...]
    if apply_ln:
        h_ref, halo_ref = rest
        x = _layer_norm(x, g_ref[...], b_ref[...])
        h_ref[...] = x
    else:
        (halo_ref,) = rest
    xb = x.astype(BF16)

    def proj(lo, hi):
        return lax.dot_general(xb, wt_ref[lo:hi, :], (((1,), (1,)), ((), ())), preferred_element_type=F32)

    uv, cg = proj(SRC_UV, SRC_CG), proj(SRC_CG, SRC_END)
    ax, ag = proj(SRC_AX, SRC_AG), proj(SRC_AG, SRC_CQ)
    n_lat = SRC_BG - SRC_CQ
    lat_ref[:, :n_lat] = proj(SRC_CQ, SRC_BG).astype(lat_ref.dtype)
    lat_ref[:, n_lat:] = jnp.zeros((xb.shape[0], LAT_COLS - n_lat), lat_ref.dtype)
    _sgu_mixer(uv, cg, ng_ref, nb_ref, ws_ref, bs_ref, yc_ref)
    _pool_mixer(ax, ag, pl.program_id(1), halo_ref, pw_ref, psc_ref, ya_ref)
    bg_ref[...] = proj(SRC_BG, SRC_UV).astype(bg_ref.dtype)


def _front(x3d, ln_g, ln_b, w_in_t, w_pool, pool_scale, sgu_g, sgu_b, w_s, b_s_col, l, *, apply_ln):
    b, s, _ = x3d.shape
    assert w_in_t.shape[1:] == (SRC_END, D_MODEL)
    assert TM_FRONT % SGU_CHUNK == 0 and POOL_HALO >= max(POOL_WINDOWS) - 1
    stacks = (w_in_t, w_pool, pool_scale, sgu_g, sgu_b, w_s, b_s_col)

    def rows(width):
        return pl.BlockSpec((None, TM_FRONT, width), lambda i, j: (i, j, 0))

    def out(width):
        return jax.ShapeDtypeStruct((b, s, width), BF16)

    return pl.pallas_call(
        functools.partial(_front_kernel, apply_ln=apply_ln),
        grid=(b, s // TM_FRONT),
        in_specs=[rows(D_MODEL), _resident((1, D_MODEL)), _resident((1, D_MODEL))]
                 + [_layer(p, l) for p in stacks],
        out_specs=[rows(LAT_COLS), rows(MLA_WIDTH), rows(POOL_WIDTH), rows(SGU_WIDTH)]
                  + ([rows(D_MODEL)] if apply_ln else []),
        out_shape=[out(LAT_COLS), out(MLA_WIDTH), out(POOL_WIDTH), out(SGU_WIDTH)]
                  + ([jax.ShapeDtypeStruct((b, s, D_MODEL), F32)] if apply_ln else []),
        scratch_shapes=[pltpu.VMEM((POOL_HALO, POOL_WIDTH), F32)],
        compiler_params=pltpu.CompilerParams(vmem_limit_bytes=VMEM_LIMIT,
                                             dimension_semantics=("arbitrary", "arbitrary")),
        name="front",
    )(x3d, ln_g, ln_b, *stacks)


def _mla_proj_kernel(cq_ref, ckv_ref, kr_ref, cos_ref, sin_ref, qg_ref, wqn_ref, wqr_ref,
                     kvg_ref, wk_ref, wvt_ref, qn_ref, qr_ref, kn_ref, kr_out_ref, vt_ref):
    tm = cq_ref.shape[0]
    scale = MLA_QK_DIM ** -0.5 * math.log2(math.e)
    cos = cos_ref[...]
    sin = sin_ref[...]
    lane = lax.broadcasted_iota(jnp.int32, (tm, LANES), 1)
    first_half = (lane % MLA_ROPE_DIM) < (MLA_ROPE_DIM // 2)
    low_head = lane < MLA_ROPE_DIM

    def rope(x):
        rotated = jnp.where(first_half,
                            -pltpu.roll(x, LANES - MLA_ROPE_DIM // 2, axis=1),
                            pltpu.roll(x, MLA_ROPE_DIM // 2, axis=1))
        return x * cos + rotated * sin

    cqn = _rms_norm(cq_ref[...].astype(F32), qg_ref[...]).astype(BF16)
    qn = jnp.dot(cqn, wqn_ref[...], preferred_element_type=F32) * scale
    qr = jnp.dot(cqn, wqr_ref[...], preferred_element_type=F32)
    for pair in range(MLA_HEADS // 2):
        qr_ref[0, pair] = (rope(qr[:, pair * LANES:(pair + 1) * LANES]) * scale).astype(qr_ref.dtype)
    for h in range(MLA_HEADS):
        qn_ref[0, h] = qn[:, h * MLA_NOPE_DIM:(h + 1) * MLA_NOPE_DIM].astype(qn_ref.dtype)

    ckvn = _rms_norm(ckv_ref[...].astype(F32), kvg_ref[...]).astype(BF16)
    kn = jnp.dot(ckvn, wk_ref[...], preferred_element_type=F32)
    vt = lax.dot_general(wvt_ref[...], ckvn, (((1,), (1,)), ((), ())), preferred_element_type=F32)
    kr = rope(kr_ref[...].astype(F32))
    kr_out_ref[0] = jnp.where(low_head, kr, pltpu.roll(kr, MLA_ROPE_DIM, axis=1)).astype(kr_out_ref.dtype)
    for h in range(MLA_HEADS):
        kn_ref[0, h] = kn[:, h * MLA_NOPE_DIM:(h + 1) * MLA_NOPE_DIM].astype(kn_ref.dtype)
        vt_ref[0, h] = vt[h * MLA_V_DIM:(h + 1) * MLA_V_DIM, :].astype(vt_ref.dtype)


def _mla_proj(lat3d, cos, sin, q_norm_g, wq_nope, wq_rope, kv_norm_g, wk_nope, wv_t, l):
    b, s, _ = lat3d.shape
    nb = s // TM_MLA
    stacks = (q_norm_g, wq_nope, wq_rope, kv_norm_g, wk_nope, wv_t)
    tab = pl.BlockSpec((TM_MLA, LANES), lambda i, j: (i * nb + j, 0))

    def seg(width, col):
        return pl.BlockSpec((None, TM_MLA, width), lambda i, j: (i, j, col // width))

    def heads(n):
        return pl.BlockSpec((1, n, TM_MLA, LANES), lambda i, j: (i, 0, j, 0))

    def heads_shape(n):
        return jax.ShapeDtypeStruct((b, n, s, LANES), BF16)

    return pl.pallas_call(
        _mla_proj_kernel,
        grid=(b, nb),
        in_specs=[seg(MLA_Q_RANK, LAT_CQ), seg(MLA_KV_RANK, LAT_CKV), seg(LANES, LAT_KR), tab, tab]
                 + [_layer(p, l) for p in stacks],
        out_specs=[heads(MLA_HEADS), heads(MLA_HEADS // 2), heads(MLA_HEADS),
                   pl.BlockSpec((1, TM_MLA, LANES), lambda i, j: (i, j, 0)),
                   pl.BlockSpec((1, MLA_HEADS, MLA_V_DIM, TM_MLA), lambda i, j: (i, 0, 0, j))],
        out_shape=[heads_shape(MLA_HEADS), heads_shape(MLA_HEADS // 2), heads_shape(MLA_HEADS),
                   jax.ShapeDtypeStruct((b, s, LANES), BF16),
                   jax.ShapeDtypeStruct((b, MLA_HEADS, MLA_V_DIM, s), BF16)],
        compiler_params=_params(),
        name="mla_proj",
    )(lat3d, lat3d, lat3d, cos, sin, *stacks)


def _attn_kernel(qn_ref, qr_ref, kn_ref, kr_ref, vt_ref, o_ref, s0_ref, s1_ref):
    qi = pl.program_id(1)
    n_heads = qn_ref.shape[1]
    n_chunks = kn_ref.shape[2] // TK
    hk = TK // 2
    nt_dims = (((1,), (1,)), ((), ()))
    key = lax.broadcasted_iota(jnp.int32, (hk, TQ), 0)
    qry = lax.broadcasted_iota(jnp.int32, (hk, TQ), 1)
    causal = key <= qry

    low_head = lax.broadcasted_iota(jnp.int32, (TQ, LANES), 1) < MLA_ROPE_DIM

    def col_max(s):
        return jnp.max(s, axis=0, keepdims=True)

    def q_rows(h):
        keep = low_head if h % 2 == 0 else jnp.logical_not(low_head)
        rot = jnp.where(keep, qr_ref[0, h // 2], jnp.zeros((), qr_ref.dtype))
        return jnp.concatenate([qn_ref[0, h], rot], axis=1)

    def k_rows(h, lo, hi):
        return jnp.concatenate([kn_ref[0, h, lo:hi, :], kr_ref[0, lo:hi, :]], axis=1)

    def produce(h, s_ref, nk):
        q = q_rows(h)
        s = lax.dot_general(k_rows(h, 0, nk - hk), q, nt_dims, preferred_element_type=F32)
        s_low = lax.dot_general(k_rows(h, nk - hk, nk), q[hk:], nt_dims, preferred_element_type=F32)
        top = jnp.where(causal, s[nk - TK:], -jnp.inf)
        low = jnp.where(causal[:, :hk], s_low, -jnp.inf)
        s_ref[nk - TK:nk - hk] = top
        s_ref[nk - hk:nk, hk:] = low
        m = col_max(top)
        if nk > TK:
            s_ref[:nk - TK] = s[:nk - TK]
            m = jnp.maximum(m, col_max(s[:nk - TK]))
        return jnp.concatenate([m[:, :hk], jnp.maximum(m[:, hk:], col_max(low))], axis=1)

    def consume(h, s_ref, m, nk):
        p = jnp.exp2(s_ref[:nk - hk] - m)
        p_low = jnp.exp2(s_ref[nk - hk:nk, hk:] - m[:, hk:])
        l = jnp.sum(p, axis=0, keepdims=True)
        l = jnp.concatenate([l[:, :hk], l[:, hk:] + jnp.sum(p_low, axis=0, keepdims=True)], axis=1)
        pb = p.astype(BF16)
        o_first = jnp.dot(vt_ref[0, h, :, :nk - hk], pb[:, :hk], preferred_element_type=F32)
        o_last = jnp.dot(vt_ref[0, h, :, :nk], jnp.concatenate([pb[:, hk:], p_low.astype(BF16)], axis=0),
                         preferred_element_type=F32)
        o = jnp.concatenate([o_first, o_last], axis=1) / l
        o_ref[0, h] = o.T.astype(o_ref.dtype)

    def run(nk):
        bufs = (s0_ref, s1_ref)
        m = produce(0, bufs[0], nk)
        for h in range(n_heads):
            m_next = produce(h + 1, bufs[(h + 1) % 2], nk) if h + 1 < n_heads else None
            consume(h, bufs[h % 2], m, nk)
            m = m_next

    for c in range(n_chunks):
        pl.when(qi == c)(functools.partial(run, (c + 1) * TK))


def _attention(qn, qr, kn, kr, vt):
    b, h, s, _ = qn.shape
    assert TQ == TK and h % 2 == 0
    return pl.pallas_call(
        _attn_kernel,
        grid=(b, s // TQ),
        in_specs=[pl.BlockSpec((1, h, TQ, LANES), lambda i, j: (i, 0, j, 0)),
                  pl.BlockSpec((1, h // 2, TQ, LANES), lambda i, j: (i, 0, j, 0)),
                  pl.BlockSpec((1, h, s, LANES), lambda i, j: (i, 0, 0, 0)),
                  pl.BlockSpec((1, s, LANES), lambda i, j: (i, 0, 0)),
                  pl.BlockSpec((1, h, MLA_V_DIM, s), lambda i, j: (i, 0, 0, 0))],
        out_specs=pl.BlockSpec((1, h, TQ, MLA_V_DIM), lambda i, j: (i, 0, j, 0)),
        out_shape=jax.ShapeDtypeStruct((b, h, s, MLA_V_DIM), BF16),
        scratch_shapes=[pltpu.VMEM((s, TQ), F32), pltpu.VMEM((s, TQ), F32)],
        compiler_params=_params(),
        name="mla_attention",
    )(qn, qr, kn, kr, vt)


def _out_proj_kernel(ya_ref, yb_ref, bg_ref, yc_ref, h_ref, w_ref, bo_ref, pg_ref, pb_ref, o_ref):
    attn = jnp.concatenate([yb_ref[0, h] for h in range(MLA_HEADS)], axis=1).astype(F32)
    yb = (attn * _silu(bg_ref[...].astype(F32))).astype(BF16)
    y = jnp.concatenate([ya_ref[...], yb, yc_ref[...]], axis=1)
    out = jnp.dot(y, w_ref[...], preferred_element_type=F32) + bo_ref[...]
    o_ref[...] = _layer_norm(ALPHA * h_ref[...] + out, pg_ref[...], pb_ref[...])


def _out_proj(y_a, y_b, b_gate, y_c, h3d, w_out, b_out, post_g, post_b, l):
    b, s, _ = h3d.shape

    def rows(width):
        return pl.BlockSpec((None, TM_OUT, width), lambda i, j: (i, j, 0))

    return pl.pallas_call(
        _out_proj_kernel,
        grid=(b, s // TM_OUT),
        in_specs=[rows(POOL_WIDTH),
                  pl.BlockSpec((1, MLA_HEADS, TM_OUT, MLA_V_DIM), lambda i, j: (i, 0, j, 0)),
                  rows(MLA_WIDTH), rows(SGU_WIDTH), rows(D_MODEL),
                  _layer(w_out, l), _layer(b_out, l), _layer(post_g, l), _layer(post_b, l)],
        out_specs=rows(D_MODEL),
        out_shape=jax.ShapeDtypeStruct((b, s, D_MODEL), F32),
        compiler_params=_params(),
        name="out_proj",
    )(y_a, y_b, b_gate, y_c, h3d, w_out, b_out, post_g, post_b)


def kernel(x, positions, ln_in_g, ln_in_b, w_in, pool_w, pool_scale, q_norm_g, w_uq, kv_norm_g, w_ukv,
           sgu_norm_g, sgu_norm_b, sgu_w, sgu_b, w_out, b_out, ln_post_g, ln_post_b):
    b, s, d = x.shape
    assert d == D_MODEL and s % TQ == 0 and s % TM_OUT == 0 and s % TM_FRONT == 0 and s % TM_MLA == 0

    cos, sin = _rope_tables(positions)
    ln_in_g2, ln_in_b2 = ln_in_g[None, :], ln_in_b[None, :]

    depth = w_in.shape[0]
    assert depth == DEPTH
    w_in_t = jnp.swapaxes(w_in, 1, 2).astype(BF16)
    wq = w_uq.reshape(depth, MLA_Q_RANK, MLA_HEADS, MLA_QK_DIM)
    wq_nope = wq[..., :MLA_NOPE_DIM].reshape(depth, MLA_Q_RANK, MLA_HEADS * MLA_NOPE_DIM).astype(BF16)
    wq_rope = wq[..., MLA_NOPE_DIM:].reshape(depth, MLA_Q_RANK, MLA_HEADS * MLA_ROPE_DIM).astype(BF16)
    wkv = w_ukv.reshape(depth, MLA_KV_RANK, MLA_HEADS, MLA_NOPE_DIM + MLA_V_DIM)
    wk_nope = wkv[..., :MLA_NOPE_DIM].reshape(depth, MLA_KV_RANK, MLA_HEADS * MLA_NOPE_DIM).astype(BF16)
    wv_t = wkv[..., MLA_NOPE_DIM:].reshape(depth, MLA_KV_RANK, MLA_HEADS * MLA_V_DIM).swapaxes(1, 2).astype(BF16)
    w_pool = pool_w.astype(BF16)
    w_out_b = w_out.astype(BF16)

    def row(p):
        return p[:, None, :]

    h = x
    for l in range(depth):
        first = l == 0
        lat, b_gate, y_a, y_c, *normed = _front(
            h, ln_in_g2, ln_in_b2, w_in_t, w_pool, row(pool_scale),
            row(sgu_norm_g), row(sgu_norm_b), sgu_w, sgu_b[..., None], l, apply_ln=first)
        if first:
            (h,) = normed
        y_b = _attention(*_mla_proj(lat, cos, sin, row(q_norm_g), wq_nope, wq_rope, row(kv_norm_g),
                                    wk_nope, wv_t, l))
        h = _out_proj(y_a, y_b, b_gate, y_c, h, w_out_b, row(b_out), row(ln_post_g), row(ln_post_b), l)
    return h
```

```python
import functools
import math

import jax
import jax.numpy as jnp
from jax import lax
from jax.experimental import pallas as pl
from jax.experimental.pallas import tpu as pltpu

F32 = jnp.float32
BF16 = jnp.bfloat16

D_MODEL = 2048
DEPTH = 2
EPS = 1e-5
POOL_WIDTH = 512
POOL_WINDOWS = (2, 4, 8, 16)
POOL_GROUP_DIM = 128
MLA_HEADS = 8
MLA_NOPE_DIM = 128
MLA_ROPE_DIM = 64
MLA_V_DIM = 128
MLA_WIDTH = MLA_HEADS * MLA_V_DIM
MLA_Q_RANK = 512
MLA_KV_RANK = 256
MLA_QK_DIM = MLA_NOPE_DIM + MLA_ROPE_DIM
ROPE_THETA = 10000.0
SGU_WIDTH = 512
SGU_HEADS = 4
SGU_HEAD_DIM = 128
SGU_CHUNK = 128
ALPHA = (2.0 * DEPTH) ** 0.25

LANES = 128
MXU_DIM = 256
VMEM_LIMIT = 56 * 1024 * 1024

SRC_AX = 0
SRC_AG = SRC_AX + POOL_WIDTH
SRC_CQ = SRC_AG + POOL_WIDTH
SRC_CKV = SRC_CQ + MLA_Q_RANK
SRC_KR = SRC_CKV + MLA_KV_RANK
SRC_BG = SRC_KR + MLA_ROPE_DIM
SRC_UV = SRC_BG + MLA_WIDTH
SRC_CG = SRC_UV + 2 * SGU_WIDTH
SRC_END = SRC_CG + SGU_WIDTH
LAT_CQ = 0
LAT_CKV = LAT_CQ + MLA_Q_RANK
LAT_KR = LAT_CKV + MLA_KV_RANK
LAT_COLS = LAT_KR + MXU_DIM
POOL_HALO = 16
VT_ROWS = MLA_V_DIM + 16

TM_FRONT = 512
TM_MLA = 1024
TQ = 512
TK = 512
TM_OUT = 512


def _layer_norm(x, g, b):
    mu = jnp.mean(x, axis=-1, keepdims=True)
    xc = x - mu
    var = jnp.mean(xc * xc, axis=-1, keepdims=True)
    return xc * lax.rsqrt(var + EPS) * g + b


def _rms_norm(x, g):
    ms = jnp.mean(x * x, axis=-1, keepdims=True)
    return x * lax.rsqrt(ms + EPS) * g


def _silu(x):
    return x / (1.0 + jnp.exp(-x))


def _resident(shape):
    return pl.BlockSpec(shape, lambda *_: (0,) * len(shape), pipeline_mode=pl.Buffered(1))


def _layer(stacked, l):
    tail = stacked.shape[1:]
    return pl.BlockSpec((None,) + tail, lambda *_: (l,) + (0,) * len(tail), pipeline_mode=pl.Buffered(1))


def _params():
    return pltpu.CompilerParams(vmem_limit_bytes=VMEM_LIMIT)


def _rope_table_kernel(pos_ref, freq_ref, cos_ref, sin_ref):
    half = MLA_ROPE_DIM // 2
    ang = pos_ref[...].astype(F32) * freq_ref[...]
    lane_group = lax.broadcasted_iota(jnp.int32, ang.shape, 1) // half

    def spread(x, r):
        only_r = jnp.where(lane_group == r, x, 0.0)
        return sum((pltpu.roll(only_r, k * half, axis=1) for k in range(1, LANES // half)), only_r)

    cos, sin = jnp.cos(ang), jnp.sin(ang)
    for r in range(LANES // half):
        cos_ref[r] = spread(cos, r)
        sin_ref[r] = spread(sin, r)


def _rope_tables(positions):
    t = positions.size
    half = MLA_ROPE_DIM // 2
    per_row = LANES // half
    rows = t // per_row
    inv_freq = ROPE_THETA ** (-jnp.arange(half, dtype=F32) / half)
    freq_row = jnp.tile(inv_freq, per_row)[None, :]
    pos_rows = jnp.repeat(positions.reshape(per_row, rows).T, half, axis=1)
    tm = 1024
    cos, sin = pl.pallas_call(
        _rope_table_kernel,
        grid=(rows // tm,),
        in_specs=[pl.BlockSpec((tm, LANES), lambda i: (i, 0)), _resident((1, LANES))],
        out_specs=[pl.BlockSpec((per_row, tm, LANES), lambda i: (0, i, 0))] * 2,
        out_shape=[jax.ShapeDtypeStruct((per_row, rows, LANES), F32)] * 2,
        compiler_params=_params(),
        name="rope_tables",
    )(pos_rows, freq_row)
    return cos.reshape(t, LANES), sin.reshape(t, LANES)


def _pool_mixer(ax, ag, seq_tile, halo_ref, w_ref, sc_ref, o_ref):
    tm = ax.shape[0]

    @pl.when(seq_tile == 0)
    def _():
        halo_ref[...] = jnp.zeros_like(halo_ref)

    halo = halo_ref[...]
    head = POOL_HALO
    t_head = seq_tile * tm + lax.broadcasted_iota(jnp.int32, (head, POOL_GROUP_DIM), 0)
    for gi, win in enumerate(POOL_WINDOWS):
        cols = slice(gi * POOL_GROUP_DIM, (gi + 1) * POOL_GROUP_DIM)
        x = ax[:, cols]
        acc = jnp.concatenate([halo[:, cols], x], axis=0)
        span = 1
        while span < win:
            acc = acc + pltpu.roll(acc, span, axis=0)
            span *= 2
        count_head = jnp.minimum(t_head + 1, win).astype(F32)
        mean = jnp.concatenate([acc[POOL_HALO:POOL_HALO + head] / count_head,
                                acc[POOL_HALO + head:] * (1.0 / win)], axis=0)
        d = (mean - x).astype(BF16)
        y = jnp.dot(d, w_ref[gi], preferred_element_type=F32)
        o_ref[:, cols] = (y * sc_ref[:, cols] * _silu(ag[:, cols])).astype(o_ref.dtype)
    halo_ref[...] = ax[tm - POOL_HALO:, :]


def _sgu_mixer(uv, cg, ng_ref, nb_ref, ws_ref, bs_ref, o_ref):
    tm = uv.shape[0]
    uv = 0.5 * uv * (1.0 + lax.erf(uv * (2.0 ** -0.5)))
    u = uv[:, :SGU_WIDTH]
    vb = _layer_norm(uv[:, SGU_WIDTH:], ng_ref[...], nb_ref[...]).astype(BF16)
    gate = _silu(cg)
    n_chunks = tm // SGU_CHUNK
    row = lax.broadcasted_iota(jnp.int32, (SGU_CHUNK, SGU_CHUNK), 0)
    col = lax.broadcasted_iota(jnp.int32, (SGU_CHUNK, SGU_CHUNK), 1)
    for h in range(SGU_HEADS):
        cols = slice(h * SGU_HEAD_DIM, (h + 1) * SGU_HEAD_DIM)
        w = jnp.where(col <= row, ws_ref[h], 0.0).astype(BF16)
        vh = jnp.concatenate([vb[c * SGU_CHUNK:(c + 1) * SGU_CHUNK, cols] for c in range(n_chunks)], axis=1)
        mixed = jnp.dot(w, vh, preferred_element_type=F32) + bs_ref[h]
        for c in range(n_chunks):
            rows = slice(c * SGU_CHUNK, (c + 1) * SGU_CHUNK)
            piece = mixed[:, c * SGU_HEAD_DIM:(c + 1) * SGU_HEAD_DIM]
            o_ref[rows, cols] = (u[rows, cols] * piece * gate[rows, cols]).astype(o_ref.dtype)


def _front_kernel(x_ref, g_ref, b_ref, wt_ref, pw_ref, psc_ref, ng_ref, nb_ref,
                  ws_ref, bs_ref, lat_ref, bg_ref, ya_ref, yc_ref, *rest, apply_ln):
    x = x_ref[...]
    if apply_ln:
        h_ref, halo_ref = rest
        x = _layer_norm(x, g_ref[...], b_ref[...])
        h_ref[...] = x
    else:
        (halo_ref,) = rest
    xb = x.astype(BF16)

    def proj(lo, hi):
        return lax.dot_general(xb, wt_ref[lo:hi, :], (((1,), (1,)), ((), ())), preferred_element_type=F32)

    uv, cg = proj(SRC_UV, SRC_CG), proj(SRC_CG, SRC_END)
    ax, ag = proj(SRC_AX, SRC_AG), proj(SRC_AG, SRC_CQ)
    n_lat = SRC_BG - SRC_CQ
    lat_ref[:, :n_lat] = proj(SRC_CQ, SRC_BG).astype(lat_ref.dtype)
    lat_ref[:, n_lat:] = jnp.zeros((xb.shape[0], LAT_COLS - n_lat), lat_ref.dtype)
    _sgu_mixer(uv, cg, ng_ref, nb_ref, ws_ref, bs_ref, yc_ref)
    _pool_mixer(ax, ag, pl.program_id(1), halo_ref, pw_ref, psc_ref, ya_ref)
    bg_ref[...] = proj(SRC_BG, SRC_UV).astype(bg_ref.dtype)


def _front(x3d, ln_g, ln_b, w_in_t, w_pool, pool_scale, sgu_g, sgu_b, w_s, b_s_col, l, *, apply_ln):
    b, s, _ = x3d.shape
    assert w_in_t.shape[1:] == (SRC_END, D_MODEL)
    assert TM_FRONT % SGU_CHUNK == 0 and POOL_HALO >= max(POOL_WINDOWS) - 1
    stacks = (w_in_t, w_pool, pool_scale, sgu_g, sgu_b, w_s, b_s_col)

    def rows(width):
        return pl.BlockSpec((None, TM_FRONT, width), lambda i, j: (i, j, 0))

    def out(width):
        return jax.ShapeDtypeStruct((b, s, width), BF16)

    return pl.pallas_call(
        functools.partial(_front_kernel, apply_ln=apply_ln),
        grid=(b, s // TM_FRONT),
        in_specs=[rows(D_MODEL), _resident((1, D_MODEL)), _resident((1, D_MODEL))]
                 + [_layer(p, l) for p in stacks],
        out_specs=[rows(LAT_COLS), rows(MLA_WIDTH), rows(POOL_WIDTH), rows(SGU_WIDTH)]
                  + ([rows(D_MODEL)] if apply_ln else []),
        out_shape=[out(LAT_COLS), out(MLA_WIDTH), out(POOL_WIDTH), out(SGU_WIDTH)]
                  + ([jax.ShapeDtypeStruct((b, s, D_MODEL), F32)] if apply_ln else []),
        scratch_shapes=[pltpu.VMEM((POOL_HALO, POOL_WIDTH), F32)],
        compiler_params=pltpu.CompilerParams(vmem_limit_bytes=VMEM_LIMIT,
                                             dimension_semantics=("arbitrary", "arbitrary")),
        name="front",
    )(x3d, ln_g, ln_b, *stacks)


def _mla_proj_kernel(cq_ref, ckv_ref, kr_ref, cos_ref, sin_ref, qg_ref, wqn_ref, wqr_ref,
                     kvg_ref, wk_ref, wvt_ref, qn_ref, qr_ref, kn_ref, kr_out_ref, vt_ref):
    tm = cq_ref.shape[0]
    scale = MLA_QK_DIM ** -0.5 * math.log2(math.e)
    cos = cos_ref[...]
    sin = sin_ref[...]
    lane = lax.broadcasted_iota(jnp.int32, (tm, LANES), 1)
    first_half = (lane % MLA_ROPE_DIM) < (MLA_ROPE_DIM // 2)
    low_head = lane < MLA_ROPE_DIM

    def rope(x):
        rotated = jnp.where(first_half,
                            -pltpu.roll(x, LANES - MLA_ROPE_DIM // 2, axis=1),
                            pltpu.roll(x, MLA_ROPE_DIM // 2, axis=1))
        return x * cos + rotated * sin

    cqn = _rms_norm(cq_ref[...].astype(F32), qg_ref[...]).astype(BF16)
    qn = jnp.dot(cqn, wqn_ref[...], preferred_element_type=F32) * scale
    qr = jnp.dot(cqn, wqr_ref[...], preferred_element_type=F32)
    for pair in range(MLA_HEADS // 2):
        qr_ref[0, pair] = (rope(qr[:, pair * LANES:(pair + 1) * LANES]) * scale).astype(qr_ref.dtype)
    for h in range(MLA_HEADS):
        qn_ref[0, h] = qn[:, h * MLA_NOPE_DIM:(h + 1) * MLA_NOPE_DIM].astype(qn_ref.dtype)

    ckvn = _rms_norm(ckv_ref[...].astype(F32), kvg_ref[...]).astype(BF16)
    kn = jnp.dot(ckvn, wk_ref[...], preferred_element_type=F32)
    vt = lax.dot_general(wvt_ref[...], ckvn, (((1,), (1,)), ((), ())), preferred_element_type=F32)
    kr = rope(kr_ref[...].astype(F32))
    kr_out_ref[0] = jnp.where(low_head, kr, pltpu.roll(kr, MLA_ROPE_DIM, axis=1)).astype(kr_out_ref.dtype)
    ones = jnp.ones((VT_ROWS - MLA_V_DIM, tm), vt_ref.dtype)
    for h in range(MLA_HEADS):
        kn_ref[0, h] = kn[:, h * MLA_NOPE_DIM:(h + 1) * MLA_NOPE_DIM].astype(kn_ref.dtype)
        vt_ref[0, h, :MLA_V_DIM] = vt[h * MLA_V_DIM:(h + 1) * MLA_V_DIM, :].astype(vt_ref.dtype)
        vt_ref[0, h, MLA_V_DIM:] = ones


def _mla_proj(lat3d, cos, sin, q_norm_g, wq_nope, wq_rope, kv_norm_g, wk_nope, wv_t, l):
    b, s, _ = lat3d.shape
    nb = s // TM_MLA
    stacks = (q_norm_g, wq_nope, wq_rope, kv_norm_g, wk_nope, wv_t)
    tab = pl.BlockSpec((TM_MLA, LANES), lambda i, j: (i * nb + j, 0))

    def seg(width, col):
        return pl.BlockSpec((None, TM_MLA, width), lambda i, j: (i, j, col // width))

    def heads(n):
        return pl.BlockSpec((1, n, TM_MLA, LANES), lambda i, j: (i, 0, j, 0))

    def heads_shape(n):
        return jax.ShapeDtypeStruct((b, n, s, LANES), BF16)

    return pl.pallas_call(
        _mla_proj_kernel,
        grid=(b, nb),
        in_specs=[seg(MLA_Q_RANK, LAT_CQ), seg(MLA_KV_RANK, LAT_CKV), seg(LANES, LAT_KR), tab, tab]
                 + [_layer(p, l) for p in stacks],
        out_specs=[heads(MLA_HEADS), heads(MLA_HEADS // 2), heads(MLA_HEADS),
                   pl.BlockSpec((1, TM_MLA, LANES), lambda i, j: (i, j, 0)),
                   pl.BlockSpec((1, MLA_HEADS, VT_ROWS, TM_MLA), lambda i, j: (i, 0, 0, j))],
        out_shape=[heads_shape(MLA_HEADS), heads_shape(MLA_HEADS // 2), heads_shape(MLA_HEADS),
                   jax.ShapeDtypeStruct((b, s, LANES), BF16),
                   jax.ShapeDtypeStruct((b, MLA_HEADS, VT_ROWS, s), BF16)],
        compiler_params=_params(),
        name="mla_proj",
    )(lat3d, lat3d, lat3d, cos, sin, *stacks)


def _attn_kernel(qn_ref, qr_ref, kn_ref, kr_ref, vt_ref, o_ref, s0_ref, s1_ref):
    qi = pl.program_id(1)
    n_heads = qn_ref.shape[1]
    n_chunks = kn_ref.shape[2] // TK
    hk = TK // 2
    nt_dims = (((1,), (1,)), ((), ()))
    key = lax.broadcasted_iota(jnp.int32, (hk, TQ), 0)
    qry = lax.broadcasted_iota(jnp.int32, (hk, TQ), 1)
    causal = key <= qry

    low_head = lax.broadcasted_iota(jnp.int32, (TQ, LANES), 1) < MLA_ROPE_DIM

    def col_max(s):
        return jnp.max(s, axis=0, keepdims=True)

    def q_rows(h):
        keep = low_head if h % 2 == 0 else jnp.logical_not(low_head)
        rot = jnp.where(keep, qr_ref[0, h // 2], jnp.zeros((), qr_ref.dtype))
        return jnp.concatenate([qn_ref[0, h], rot], axis=1)

    def k_rows(h, lo, hi):
        return jnp.concatenate([kn_ref[0, h, lo:hi, :], kr_ref[0, lo:hi, :]], axis=1)

    def produce(h, s_ref, nk):
        q = q_rows(h)
        s = lax.dot_general(k_rows(h, 0, nk - hk), q, nt_dims, preferred_element_type=F32)
        s_low = lax.dot_general(k_rows(h, nk - hk, nk), q[hk:], nt_dims, preferred_element_type=F32)
        top = jnp.where(causal, s[nk - TK:], -jnp.inf)
        low = jnp.where(causal[:, :hk], s_low, -jnp.inf)
        s_ref[nk - TK:nk - hk] = top
        s_ref[nk - hk:nk, hk:] = low
        m = col_max(top)
        if nk > TK:
            s_ref[:nk - TK] = s[:nk - TK]
            m = jnp.maximum(m, col_max(s[:nk - TK]))
        return jnp.concatenate([m[:, :hk], jnp.maximum(m[:, hk:], col_max(low))], axis=1)

    def consume(h, s_ref, m, nk):
        pb = jnp.exp2(s_ref[:nk - hk] - m).astype(BF16)
        p_low = jnp.exp2(s_ref[nk - hk:nk, hk:] - m[:, hk:]).astype(BF16)
        o_first = jnp.dot(vt_ref[0, h, :, :nk - hk], pb[:, :hk], preferred_element_type=F32)
        o_last = jnp.dot(vt_ref[0, h, :, :nk], jnp.concatenate([pb[:, hk:], p_low], axis=0),
                         preferred_element_type=F32)
        o = jnp.concatenate([o_first, o_last], axis=1)
        o = o[:MLA_V_DIM] / o[MLA_V_DIM:MLA_V_DIM + 1]
        o_ref[0, h] = o.T.astype(o_ref.dtype)

    def run(nk):
        bufs = (s0_ref, s1_ref)
        m = produce(0, bufs[0], nk)
        for h in range(n_heads):
            m_next = produce(h + 1, bufs[(h + 1) % 2], nk) if h + 1 < n_heads else None
            consume(h, bufs[h % 2], m, nk)
            m = m_next

    for c in range(n_chunks):
        pl.when(qi == c)(functools.partial(run, (c + 1) * TK))


def _attention(qn, qr, kn, kr, vt):
    b, h, s, _ = qn.shape
    assert TQ == TK and h % 2 == 0
    return pl.pallas_call(
        _attn_kernel,
        grid=(b, s // TQ),
        in_specs=[pl.BlockSpec((1, h, TQ, LANES), lambda i, j: (i, 0, j, 0)),
                  pl.BlockSpec((1, h // 2, TQ, LANES), lambda i, j: (i, 0, j, 0)),
                  pl.BlockSpec((1, h, s, LANES), lambda i, j: (i, 0, 0, 0)),
                  pl.BlockSpec((1, s, LANES), lambda i, j: (i, 0, 0)),
                  pl.BlockSpec((1, h, VT_ROWS, s), lambda i, j: (i, 0, 0, 0))],
        out_specs=pl.BlockSpec((1, h, TQ, MLA_V_DIM), lambda i, j: (i, 0, j, 0)),
        out_shape=jax.ShapeDtypeStruct((b, h, s, MLA_V_DIM), BF16),
        scratch_shapes=[pltpu.VMEM((s, TQ), F32), pltpu.VMEM((s, TQ), F32)],
        compiler_params=_params(),
        name="mla_attention",
    )(qn, qr, kn, kr, vt)


def _out_proj_kernel(ya_ref, yb_ref, bg_ref, yc_ref, h_ref, w_ref, bo_ref, pg_ref, pb_ref, o_ref):
    attn = jnp.concatenate([yb_ref[0, h] for h in range(MLA_HEADS)], axis=1).astype(F32)
    yb = (attn * _silu(bg_ref[...].astype(F32))).astype(BF16)
    y = jnp.concatenate([ya_ref[...], yb, yc_ref[...]], axis=1)
    out = jnp.dot(y, w_ref[...], preferred_element_type=F32) + bo_ref[...]
    o_ref[...] = _layer_norm(ALPHA * h_ref[...] + out, pg_ref[...], pb_ref[...])


def _out_proj(y_a, y_b, b_gate, y_c, h3d, w_out, b_out, post_g, post_b, l):
    b, s, _ = h3d.shape

    def rows(width):
        return pl.BlockSpec((None, TM_OUT, width), lambda i, j: (i, j, 0))

    return pl.pallas_call(
        _out_proj_kernel,
        grid=(b, s // TM_OUT),
        in_specs=[rows(POOL_WIDTH),
                  pl.BlockSpec((1, MLA_HEADS, TM_OUT, MLA_V_DIM), lambda i, j: (i, 0, j, 0)),
                  rows(MLA_WIDTH), rows(SGU_WIDTH), rows(D_MODEL),
                  _layer(w_out, l), _layer(b_out, l), _layer(post_g, l), _layer(post_b, l)],
        out_specs=rows(D_MODEL),
        out_shape=jax.ShapeDtypeStruct((b, s, D_MODEL), F32),
        compiler_params=_params(),
        name="out_proj",
    )(y_a, y_b, b_gate, y_c, h3d, w_out, b_out, post_g, post_b)


def kernel(x, positions, ln_in_g, ln_in_b, w_in, pool_w, pool_scale, q_norm_g, w_uq, kv_norm_g, w_ukv,
           sgu_norm_g, sgu_norm_b, sgu_w, sgu_b, w_out, b_out, ln_post_g, ln_post_b):
    b, s, d = x.shape
    assert d == D_MODEL and s % TQ == 0 and s % TM_OUT == 0 and s % TM_FRONT == 0 and s % TM_MLA == 0

    cos, sin = _rope_tables(positions)
    ln_in_g2, ln_in_b2 = ln_in_g[None, :], ln_in_b[None, :]

    depth = w_in.shape[0]
    assert depth == DEPTH
    w_in_t = jnp.swapaxes(w_in, 1, 2).astype(BF16)
    wq = w_uq.reshape(depth, MLA_Q_RANK, MLA_HEADS, MLA_QK_DIM)
    wq_nope = wq[..., :MLA_NOPE_DIM].reshape(depth, MLA_Q_RANK, MLA_HEADS * MLA_NOPE_DIM).astype(BF16)
    wq_rope = wq[..., MLA_NOPE_DIM:].reshape(depth, MLA_Q_RANK, MLA_HEADS * MLA_ROPE_DIM).astype(BF16)
    wkv = w_ukv.reshape(depth, MLA_KV_RANK, MLA_HEADS, MLA_NOPE_DIM + MLA_V_DIM)
    wk_nope = wkv[..., :MLA_NOPE_DIM].reshape(depth, MLA_KV_RANK, MLA_HEADS * MLA_NOPE_DIM).astype(BF16)
    wv_t = wkv[..., MLA_NOPE_DIM:].reshape(depth, MLA_KV_RANK, MLA_HEADS * MLA_V_DIM).swapaxes(1, 2).astype(BF16)
    w_pool = pool_w.astype(BF16)
    w_out_b = w_out.astype(BF16)

    def row(p):
        return p[:, None, :]

    h = x
    for l in range(depth):
        first = l == 0
        lat, b_gate, y_a, y_c, *normed = _front(
            h, ln_in_g2, ln_in_b2, w_in_t, w_pool, row(pool_scale),
            row(sgu_norm_g), row(sgu_norm_b), sgu_w, sgu_b[..., None], l, apply_ln=first)
        if first:
            (h,) = normed
        y_b = _attention(*_mla_proj(lat, cos, sin, row(q_norm_g), wq_nope, wq_rope, row(kv_norm_g),
                                    wk_nope, wv_t, l))
        h = _out_proj(y_a, y_b, b_gate, y_c, h, w_out_b, row(b_out), row(ln_post_g), row(ln_post_b), l)
    return h
```

```python
import functools
import math

import jax
import jax.numpy as jnp
from jax import lax
from jax.experimental import pallas as pl
from jax.experimental.pallas import tpu as pltpu

F32 = jnp.float32
BF16 = jnp.bfloat16

D_MODEL = 2048
DEPTH = 2
EPS = 1e-5
POOL_WIDTH = 512
POOL_WINDOWS = (2, 4, 8, 16)
POOL_GROUP_DIM = 128
MLA_HEADS = 8
MLA_NOPE_DIM = 128
MLA_ROPE_DIM = 64
MLA_V_DIM = 128
MLA_WIDTH = MLA_HEADS * MLA_V_DIM
MLA_Q_RANK = 512
MLA_KV_RANK = 256
MLA_QK_DIM = MLA_NOPE_DIM + MLA_ROPE_DIM
ROPE_THETA = 10000.0
SGU_WIDTH = 512
SGU_HEADS = 4
SGU_HEAD_DIM = 128
SGU_CHUNK = 128
ALPHA = (2.0 * DEPTH) ** 0.25

LANES = 128
MXU_DIM = 256
VMEM_LIMIT = 56 * 1024 * 1024

SRC_AX = 0
SRC_AG = SRC_AX + POOL_WIDTH
SRC_CQ = SRC_AG + POOL_WIDTH
SRC_CKV = SRC_CQ + MLA_Q_RANK
SRC_KR = SRC_CKV + MLA_KV_RANK
SRC_BG = SRC_KR + MLA_ROPE_DIM
SRC_UV = SRC_BG + MLA_WIDTH
SRC_CG = SRC_UV + 2 * SGU_WIDTH
SRC_END = SRC_CG + SGU_WIDTH
LAT_CQ = 0
LAT_CKV = LAT_CQ + MLA_Q_RANK
LAT_KR = LAT_CKV + MLA_KV_RANK
LAT_COLS = LAT_KR + MXU_DIM
POOL_HALO = 16
VT_ROWS = MLA_V_DIM + 16

TM_FRONT = 512
TM_MLA = 1024
TQ = 512
TK = 512
N_SCORE_BUFS = 3
TM_OUT = 512


def _layer_norm(x, g, b):
    mu = jnp.mean(x, axis=-1, keepdims=True)
    xc = x - mu
    var = jnp.mean(xc * xc, axis=-1, keepdims=True)
    return xc * lax.rsqrt(var + EPS) * g + b


def _rms_norm(x, g):
    ms = jnp.mean(x * x, axis=-1, keepdims=True)
    return x * lax.rsqrt(ms + EPS) * g


def _silu(x):
    return x / (1.0 + jnp.exp(-x))


def _resident(shape):
    return pl.BlockSpec(shape, lambda *_: (0,) * len(shape), pipeline_mode=pl.Buffered(1))


def _layer(stacked, l):
    tail = stacked.shape[1:]
    return pl.BlockSpec((None,) + tail, lambda *_: (l,) + (0,) * len(tail), pipeline_mode=pl.Buffered(1))


def _params():
    return pltpu.CompilerParams(vmem_limit_bytes=VMEM_LIMIT)


def _rope_table_kernel(pos_ref, freq_ref, cos_ref, sin_ref):
    half = MLA_ROPE_DIM // 2
    ang = pos_ref[...].astype(F32) * freq_ref[...]
    lane_group = lax.broadcasted_iota(jnp.int32, ang.shape, 1) // half

    def spread(x, r):
        only_r = jnp.where(lane_group == r, x, 0.0)
        return sum((pltpu.roll(only_r, k * half, axis=1) for k in range(1, LANES // half)), only_r)

    cos, sin = jnp.cos(ang), jnp.sin(ang)
    for r in range(LANES // half):
        cos_ref[r] = spread(cos, r)
        sin_ref[r] = spread(sin, r)


def _rope_tables(positions):
    t = positions.size
    half = MLA_ROPE_DIM // 2
    per_row = LANES // half
    rows = t // per_row
    inv_freq = ROPE_THETA ** (-jnp.arange(half, dtype=F32) / half)
    freq_row = jnp.tile(inv_freq, per_row)[None, :]
    pos_rows = jnp.repeat(positions.reshape(per_row, rows).T, half, axis=1)
    tm = 1024
    cos, sin = pl.pallas_call(
        _rope_table_kernel,
        grid=(rows // tm,),
        in_specs=[pl.BlockSpec((tm, LANES), lambda i: (i, 0)), _resident((1, LANES))],
        out_specs=[pl.BlockSpec((per_row, tm, LANES), lambda i: (0, i, 0))] * 2,
        out_shape=[jax.ShapeDtypeStruct((per_row, rows, LANES), F32)] * 2,
        compiler_params=_params(),
        name="rope_tables",
    )(pos_rows, freq_row)
    return cos.reshape(t, LANES), sin.reshape(t, LANES)


def _pool_mixer(ax, ag, seq_tile, halo_ref, w_ref, sc_ref, o_ref):
    tm = ax.shape[0]

    @pl.when(seq_tile == 0)
    def _():
        halo_ref[...] = jnp.zeros_like(halo_ref)

    halo = halo_ref[...]
    head = POOL_HALO
    t_head = seq_tile * tm + lax.broadcasted_iota(jnp.int32, (head, POOL_GROUP_DIM), 0)
    for gi, win in enumerate(POOL_WINDOWS):
        cols = slice(gi * POOL_GROUP_DIM, (gi + 1) * POOL_GROUP_DIM)
        x = ax[:, cols]
        acc = jnp.concatenate([halo[:, cols], x], axis=0)
        span = 1
        while span < win:
            acc = acc + pltpu.roll(acc, span, axis=0)
            span *= 2
        count_head = jnp.minimum(t_head + 1, win).astype(F32)
        mean = jnp.concatenate([acc[POOL_HALO:POOL_HALO + head] / count_head,
                                acc[POOL_HALO + head:] * (1.0 / win)], axis=0)
        d = (mean - x).astype(BF16)
        y = jnp.dot(d, w_ref[gi], preferred_element_type=F32)
        o_ref[:, cols] = (y * sc_ref[:, cols] * _silu(ag[:, cols])).astype(o_ref.dtype)
    halo_ref[...] = ax[tm - POOL_HALO:, :]


def _sgu_mixer(uv, cg, ng_ref, nb_ref, ws_ref, bs_ref, o_ref):
    tm = uv.shape[0]
    uv = 0.5 * uv * (1.0 + lax.erf(uv * (2.0 ** -0.5)))
    u = uv[:, :SGU_WIDTH]
    vb = _layer_norm(uv[:, SGU_WIDTH:], ng_ref[...], nb_ref[...]).astype(BF16)
    gate = _silu(cg)
    n_chunks = tm // SGU_CHUNK
    row = lax.broadcasted_iota(jnp.int32, (SGU_CHUNK, SGU_CHUNK), 0)
    col = lax.broadcasted_iota(jnp.int32, (SGU_CHUNK, SGU_CHUNK), 1)
    for h in range(SGU_HEADS):
        cols = slice(h * SGU_HEAD_DIM, (h + 1) * SGU_HEAD_DIM)
        w = jnp.where(col <= row, ws_ref[h], 0.0).astype(BF16)
        vh = jnp.concatenate([vb[c * SGU_CHUNK:(c + 1) * SGU_CHUNK, cols] for c in range(n_chunks)], axis=1)
        mixed = jnp.dot(w, vh, preferred_element_type=F32) + bs_ref[h]
        for c in range(n_chunks):
            rows = slice(c * SGU_CHUNK, (c + 1) * SGU_CHUNK)
            piece = mixed[:, c * SGU_HEAD_DIM:(c + 1) * SGU_HEAD_DIM]
            o_ref[rows, cols] = (u[rows, cols] * piece * gate[rows, cols]).astype(o_ref.dtype)


def _front_kernel(x_ref, g_ref, b_ref, wt_ref, pw_ref, psc_ref, ng_ref, nb_ref,
                  ws_ref, bs_ref, lat_ref, bg_ref, ya_ref, yc_ref, *rest, apply_ln):
    x = x_ref[...]
    if apply_ln:
        h_ref, halo_ref = rest
        x = _layer_norm(x, g_ref[...], b_ref[...])
        h_ref[...] = x
    else:
        (halo_ref,) = rest
    xb = x.astype(BF16)

    def proj(lo, hi):
        return lax.dot_general(xb, wt_ref[lo:hi, :], (((1,), (1,)), ((), ())), preferred_element_type=F32)

    uv, cg = proj(SRC_UV, SRC_CG), proj(SRC_CG, SRC_END)
    ax, ag = proj(SRC_AX, SRC_AG), proj(SRC_AG, SRC_CQ)
    n_lat = SRC_BG - SRC_CQ
    lat_ref[:, :n_lat] = proj(SRC_CQ, SRC_BG).astype(lat_ref.dtype)
    lat_ref[:, n_lat:] = jnp.zeros((xb.shape[0], LAT_COLS - n_lat), lat_ref.dtype)
    _sgu_mixer(uv, cg, ng_ref, nb_ref, ws_ref, bs_ref, yc_ref)
    _pool_mixer(ax, ag, pl.program_id(1), halo_ref, pw_ref, psc_ref, ya_ref)
    bg_ref[...] = proj(SRC_BG, SRC_UV).astype(bg_ref.dtype)


def _front(x3d, ln_g, ln_b, w_in_t, w_pool, pool_scale, sgu_g, sgu_b, w_s, b_s_col, l, *, apply_ln):
    b, s, _ = x3d.shape
    assert w_in_t.shape[1:] == (SRC_END, D_MODEL)
    assert TM_FRONT % SGU_CHUNK == 0 and POOL_HALO >= max(POOL_WINDOWS) - 1
    stacks = (w_in_t, w_pool, pool_scale, sgu_g, sgu_b, w_s, b_s_col)

    def rows(width):
        return pl.BlockSpec((None, TM_FRONT, width), lambda i, j: (i, j, 0))

    def out(width):
        return jax.ShapeDtypeStruct((b, s, width), BF16)

    return pl.pallas_call(
        functools.partial(_front_kernel, apply_ln=apply_ln),
        grid=(b, s // TM_FRONT),
        in_specs=[rows(D_MODEL), _resident((1, D_MODEL)), _resident((1, D_MODEL))]
                 + [_layer(p, l) for p in stacks],
        out_specs=[rows(LAT_COLS), rows(MLA_WIDTH), rows(POOL_WIDTH), rows(SGU_WIDTH)]
                  + ([rows(D_MODEL)] if apply_ln else []),
        out_shape=[out(LAT_COLS), out(MLA_WIDTH), out(POOL_WIDTH), out(SGU_WIDTH)]
                  + ([jax.ShapeDtypeStruct((b, s, D_MODEL), F32)] if apply_ln else []),
        scratch_shapes=[pltpu.VMEM((POOL_HALO, POOL_WIDTH), F32)],
        compiler_params=pltpu.CompilerParams(vmem_limit_bytes=VMEM_LIMIT,
                                             dimension_semantics=("arbitrary", "arbitrary")),
        name="front",
    )(x3d, ln_g, ln_b, *stacks)


def _mla_proj_kernel(cq_ref, ckv_ref, kr_ref, cos_ref, sin_ref, qg_ref, wqn_ref, wqr_ref,
                     kvg_ref, wk_ref, wvt_ref, qn_ref, qr_ref, kn_ref, kr_out_ref, vt_ref):
    tm = cq_ref.shape[0]
    scale = MLA_QK_DIM ** -0.5 * math.log2(math.e)
    cos = cos_ref[...]
    sin = sin_ref[...]
    lane = lax.broadcasted_iota(jnp.int32, (tm, LANES), 1)
    first_half = (lane % MLA_ROPE_DIM) < (MLA_ROPE_DIM // 2)
    low_head = lane < MLA_ROPE_DIM

    def rope(x):
        rotated = jnp.where(first_half,
                            -pltpu.roll(x, LANES - MLA_ROPE_DIM // 2, axis=1),
                            pltpu.roll(x, MLA_ROPE_DIM // 2, axis=1))
        return x * cos + rotated * sin

    cqn = _rms_norm(cq_ref[...].astype(F32), qg_ref[...]).astype(BF16)
    qn = jnp.dot(cqn, wqn_ref[...], preferred_element_type=F32) * scale
    qr = jnp.dot(cqn, wqr_ref[...], preferred_element_type=F32)
    for pair in range(MLA_HEADS // 2):
        qr_ref[0, pair] = (rope(qr[:, pair * LANES:(pair + 1) * LANES]) * scale).astype(qr_ref.dtype)
    for h in range(MLA_HEADS):
        qn_ref[0, h] = qn[:, h * MLA_NOPE_DIM:(h + 1) * MLA_NOPE_DIM].astype(qn_ref.dtype)

    ckvn = _rms_norm(ckv_ref[...].astype(F32), kvg_ref[...]).astype(BF16)
    kn = jnp.dot(ckvn, wk_ref[...], preferred_element_type=F32)
    vt = lax.dot_general(wvt_ref[...], ckvn, (((1,), (1,)), ((), ())), preferred_element_type=F32)
    kr = rope(kr_ref[...].astype(F32))
    kr_out_ref[0] = jnp.where(low_head, kr, pltpu.roll(kr, MLA_ROPE_DIM, axis=1)).astype(kr_out_ref.dtype)
    ones = jnp.ones((VT_ROWS - MLA_V_DIM, tm), vt_ref.dtype)
    for h in range(MLA_HEADS):
        kn_ref[0, h] = kn[:, h * MLA_NOPE_DIM:(h + 1) * MLA_NOPE_DIM].astype(kn_ref.dtype)
        vt_ref[0, h, :MLA_V_DIM] = vt[h * MLA_V_DIM:(h + 1) * MLA_V_DIM, :].astype(vt_ref.dtype)
        vt_ref[0, h, MLA_V_DIM:] = ones


def _mla_proj(lat3d, cos, sin, q_norm_g, wq_nope, wq_rope, kv_norm_g, wk_nope, wv_t, l):
    b, s, _ = lat3d.shape
    nb = s // TM_MLA
    stacks = (q_norm_g, wq_nope, wq_rope, kv_norm_g, wk_nope, wv_t)
    tab = pl.BlockSpec((TM_MLA, LANES), lambda i, j: (i * nb + j, 0))

    def seg(width, col):
        return pl.BlockSpec((None, TM_MLA, width), lambda i, j: (i, j, col // width))

    def heads(n):
        return pl.BlockSpec((1, n, TM_MLA, LANES), lambda i, j: (i, 0, j, 0))

    def heads_shape(n):
        return jax.ShapeDtypeStruct((b, n, s, LANES), BF16)

    return pl.pallas_call(
        _mla_proj_kernel,
        grid=(b, nb),
        in_specs=[seg(MLA_Q_RANK, LAT_CQ), seg(MLA_KV_RANK, LAT_CKV), seg(LANES, LAT_KR), tab, tab]
                 + [_layer(p, l) for p in stacks],
        out_specs=[heads(MLA_HEADS), heads(MLA_HEADS // 2), heads(MLA_HEADS),
                   pl.BlockSpec((1, TM_MLA, LANES), lambda i, j: (i, j, 0)),
                   pl.BlockSpec((1, MLA_HEADS, VT_ROWS, TM_MLA), lambda i, j: (i, 0, 0, j))],
        out_shape=[heads_shape(MLA_HEADS), heads_shape(MLA_HEADS // 2), heads_shape(MLA_HEADS),
                   jax.ShapeDtypeStruct((b, s, LANES), BF16),
                   jax.ShapeDtypeStruct((b, MLA_HEADS, VT_ROWS, s), BF16)],
        compiler_params=_params(),
        name="mla_proj",
    )(lat3d, lat3d, lat3d, cos, sin, *stacks)


def _attn_kernel(qn_ref, qr_ref, kn_ref, kr_ref, vt_ref, o_ref, *bufs):
    qi = pl.program_id(1)
    n_heads = qn_ref.shape[1]
    n_chunks = kn_ref.shape[2] // TK
    hk = TK // 2
    nt_dims = (((1,), (1,)), ((), ()))
    key = lax.broadcasted_iota(jnp.int32, (hk, TQ), 0)
    qry = lax.broadcasted_iota(jnp.int32, (hk, TQ), 1)
    causal = key <= qry

    low_head = lax.broadcasted_iota(jnp.int32, (TQ, LANES), 1) < MLA_ROPE_DIM

    def col_max(s):
        return jnp.max(s, axis=0, keepdims=True)

    def q_rows(h):
        keep = low_head if h % 2 == 0 else jnp.logical_not(low_head)
        rot = jnp.where(keep, qr_ref[0, h // 2], jnp.zeros((), qr_ref.dtype))
        return jnp.concatenate([qn_ref[0, h], rot], axis=1)

    def k_rows(h, lo, hi):
        return jnp.concatenate([kn_ref[0, h, lo:hi, :], kr_ref[0, lo:hi, :]], axis=1)

    def produce(h, s_ref, nk):
        q = q_rows(h)
        s = lax.dot_general(k_rows(h, 0, nk - hk), q, nt_dims, preferred_element_type=F32)
        s_low = lax.dot_general(k_rows(h, nk - hk, nk), q[hk:], nt_dims, preferred_element_type=F32)
        top = jnp.where(causal, s[nk - TK:], -jnp.inf)
        low = jnp.where(causal[:, :hk], s_low, -jnp.inf)
        s_ref[nk - TK:nk - hk] = top
        s_ref[nk - hk:nk, hk:] = low
        m = col_max(top)
        if nk > TK:
            s_ref[:nk - TK] = s[:nk - TK]
            m = jnp.maximum(m, col_max(s[:nk - TK]))
        return jnp.concatenate([m[:, :hk], jnp.maximum(m[:, hk:], col_max(low))], axis=1)

    def consume(h, s_ref, m, nk):
        pb = jnp.exp2(s_ref[:nk - hk] - m).astype(BF16)
        p_low = jnp.exp2(s_ref[nk - hk:nk, hk:] - m[:, hk:]).astype(BF16)
        o_first = jnp.dot(vt_ref[0, h, :, :nk - hk], pb[:, :hk], preferred_element_type=F32)
        o_last = jnp.dot(vt_ref[0, h, :, :nk], jnp.concatenate([pb[:, hk:], p_low], axis=0),
                         preferred_element_type=F32)
        o = jnp.concatenate([o_first, o_last], axis=1)
        o = o[:MLA_V_DIM] / o[MLA_V_DIM:MLA_V_DIM + 1]
        o_ref[0, h] = o.T.astype(o_ref.dtype)

    def run(nk):
        ahead = len(bufs) - 1
        maxima = [produce(h, bufs[h % len(bufs)], nk) for h in range(min(ahead, n_heads))]
        for h in range(n_heads):
            if h + ahead < n_heads:
                maxima.append(produce(h + ahead, bufs[(h + ahead) % len(bufs)], nk))
            consume(h, bufs[h % len(bufs)], maxima[h], nk)

    for c in range(n_chunks):
        pl.when(qi == c)(functools.partial(run, (c + 1) * TK))


def _attention(qn, qr, kn, kr, vt):
    b, h, s, _ = qn.shape
    assert TQ == TK and h % 2 == 0
    return pl.pallas_call(
        _attn_kernel,
        grid=(b, s // TQ),
        in_specs=[pl.BlockSpec((1, h, TQ, LANES), lambda i, j: (i, 0, j, 0)),
                  pl.BlockSpec((1, h // 2, TQ, LANES), lambda i, j: (i, 0, j, 0)),
                  pl.BlockSpec((1, h, s, LANES), lambda i, j: (i, 0, 0, 0)),
                  pl.BlockSpec((1, s, LANES), lambda i, j: (i, 0, 0)),
                  pl.BlockSpec((1, h, VT_ROWS, s), lambda i, j: (i, 0, 0, 0))],
        out_specs=pl.BlockSpec((1, h, TQ, MLA_V_DIM), lambda i, j: (i, 0, j, 0)),
        out_shape=jax.ShapeDtypeStruct((b, h, s, MLA_V_DIM), BF16),
        scratch_shapes=[pltpu.VMEM((s, TQ), F32)] * N_SCORE_BUFS,
        compiler_params=_params(),
        name="mla_attention",
    )(qn, qr, kn, kr, vt)


def _out_proj_kernel(ya_ref, yb_ref, bg_ref, yc_ref, h_ref, w_ref, bo_ref, pg_ref, pb_ref, o_ref):
    attn = jnp.concatenate([yb_ref[0, h] for h in range(MLA_HEADS)], axis=1).astype(F32)
    yb = (attn * _silu(bg_ref[...].astype(F32))).astype(BF16)
    y = jnp.concatenate([ya_ref[...], yb, yc_ref[...]], axis=1)
    out = jnp.dot(y, w_ref[...], preferred_element_type=F32) + bo_ref[...]
    o_ref[...] = _layer_norm(ALPHA * h_ref[...] + out, pg_ref[...], pb_ref[...])


def _out_proj(y_a, y_b, b_gate, y_c, h3d, w_out, b_out, post_g, post_b, l):
    b, s, _ = h3d.shape

    def rows(width):
        return pl.BlockSpec((None, TM_OUT, width), lambda i, j: (i, j, 0))

    return pl.pallas_call(
        _out_proj_kernel,
        grid=(b, s // TM_OUT),
        in_specs=[rows(POOL_WIDTH),
                  pl.BlockSpec((1, MLA_HEADS, TM_OUT, MLA_V_DIM), lambda i, j: (i, 0, j, 0)),
                  rows(MLA_WIDTH), rows(SGU_WIDTH), rows(D_MODEL),
                  _layer(w_out, l), _layer(b_out, l), _layer(post_g, l), _layer(post_b, l)],
        out_specs=rows(D_MODEL),
        out_shape=jax.ShapeDtypeStruct((b, s, D_MODEL), F32),
        compiler_params=_params(),
        name="out_proj",
    )(y_a, y_b, b_gate, y_c, h3d, w_out, b_out, post_g, post_b)


def kernel(x, positions, ln_in_g, ln_in_b, w_in, pool_w, pool_scale, q_norm_g, w_uq, kv_norm_g, w_ukv,
           sgu_norm_g, sgu_norm_b, sgu_w, sgu_b, w_out, b_out, ln_post_g, ln_post_b):
    b, s, d = x.shape
    assert d == D_MODEL and s % TQ == 0 and s % TM_OUT == 0 and s % TM_FRONT == 0 and s % TM_MLA == 0

    cos, sin = _rope_tables(positions)
    ln_in_g2, ln_in_b2 = ln_in_g[None, :], ln_in_b[None, :]

    depth = w_in.shape[0]
    assert depth == DEPTH
    w_in_t = jnp.swapaxes(w_in, 1, 2).astype(BF16)
    wq = w_uq.reshape(depth, MLA_Q_RANK, MLA_HEADS, MLA_QK_DIM)
    wq_nope = wq[..., :MLA_NOPE_DIM].reshape(depth, MLA_Q_RANK, MLA_HEADS * MLA_NOPE_DIM).astype(BF16)
    wq_rope = wq[..., MLA_NOPE_DIM:].reshape(depth, MLA_Q_RANK, MLA_HEADS * MLA_ROPE_DIM).astype(BF16)
    wkv = w_ukv.reshape(depth, MLA_KV_RANK, MLA_HEADS, MLA_NOPE_DIM + MLA_V_DIM)
    wk_nope = wkv[..., :MLA_NOPE_DIM].reshape(depth, MLA_KV_RANK, MLA_HEADS * MLA_NOPE_DIM).astype(BF16)
    wv_t = wkv[..., MLA_NOPE_DIM:].reshape(depth, MLA_KV_RANK, MLA_HEADS * MLA_V_DIM).swapaxes(1, 2).astype(BF16)
    w_pool = pool_w.astype(BF16)
    w_out_b = w_out.astype(BF16)

    def row(p):
        return p[:, None, :]

    h = x
    for l in range(depth):
        first = l == 0
        lat, b_gate, y_a, y_c, *normed = _front(
            h, ln_in_g2, ln_in_b2, w_in_t, w_pool, row(pool_scale),
            row(sgu_norm_g), row(sgu_norm_b), sgu_w, sgu_b[..., None], l, apply_ln=first)
        if first:
            (h,) = normed
        y_b = _attention(*_mla_proj(lat, cos, sin, row(q_norm_g), wq_nope, wq_rope, row(kv_norm_g),
                                    wk_nope, wv_t, l))
        h = _out_proj(y_a, y_b, b_gate, y_c, h, w_out_b, row(b_out), row(ln_post_g), row(ln_post_b), l)
    return h
```

```python
import functools
import math

import jax
import jax.numpy as jnp
from jax import lax
from jax.experimental import pallas as pl
from jax.experimental.pallas import tpu as pltpu

F32 = jnp.float32
BF16 = jnp.bfloat16

D_MODEL = 2048
DEPTH = 2
EPS = 1e-5
POOL_WIDTH = 512
POOL_WINDOWS = (2, 4, 8, 16)
POOL_GROUP_DIM = 128
MLA_HEADS = 8
MLA_NOPE_DIM = 128
MLA_ROPE_DIM = 64
MLA_V_DIM = 128
MLA_WIDTH = MLA_HEADS * MLA_V_DIM
MLA_Q_RANK = 512
MLA_KV_RANK = 256
MLA_QK_DIM = MLA_NOPE_DIM + MLA_ROPE_DIM
ROPE_THETA = 10000.0
SGU_WIDTH = 512
SGU_HEADS = 4
SGU_HEAD_DIM = 128
SGU_CHUNK = 128
ALPHA = (2.0 * DEPTH) ** 0.25

LANES = 128
MXU_DIM = 256
VMEM_LIMIT = 56 * 1024 * 1024

SRC_AX = 0
SRC_AG = SRC_AX + POOL_WIDTH
SRC_CQ = SRC_AG + POOL_WIDTH
SRC_CKV = SRC_CQ + MLA_Q_RANK
SRC_KR = SRC_CKV + MLA_KV_RANK
SRC_BG = SRC_KR + MLA_ROPE_DIM
SRC_UV = SRC_BG + MLA_WIDTH
SRC_CG = SRC_UV + 2 * SGU_WIDTH
SRC_END = SRC_CG + SGU_WIDTH
LAT_CQ = 0
LAT_CKV = LAT_CQ + MLA_Q_RANK
LAT_KR = LAT_CKV + MLA_KV_RANK
LAT_COLS = LAT_KR + MXU_DIM
POOL_HALO = 16
VT_ROWS = MLA_V_DIM + 16

TM_FRONT = 512
TQ = 512
TK = 512
N_SCORE_BUFS = 3
TM_OUT = 512


def _layer_norm(x, g, b):
    mu = jnp.mean(x, axis=-1, keepdims=True)
    xc = x - mu
    var = jnp.mean(xc * xc, axis=-1, keepdims=True)
    return xc * lax.rsqrt(var + EPS) * g + b


def _rms_norm(x, g):
    ms = jnp.mean(x * x, axis=-1, keepdims=True)
    return x * lax.rsqrt(ms + EPS) * g


def _silu(x):
    return x / (1.0 + jnp.exp(-x))


def _resident(shape):
    return pl.BlockSpec(shape, lambda *_: (0,) * len(shape), pipeline_mode=pl.Buffered(1))


def _layer(stacked, l):
    tail = stacked.shape[1:]
    return pl.BlockSpec((None,) + tail, lambda *_: (l,) + (0,) * len(tail), pipeline_mode=pl.Buffered(1))


def _params():
    return pltpu.CompilerParams(vmem_limit_bytes=VMEM_LIMIT)


def _rope_table_kernel(pos_ref, freq_ref, cos_ref, sin_ref):
    half = MLA_ROPE_DIM // 2
    ang = pos_ref[...].astype(F32) * freq_ref[...]
    lane_group = lax.broadcasted_iota(jnp.int32, ang.shape, 1) // half

    def spread(x, r):
        only_r = jnp.where(lane_group == r, x, 0.0)
        return sum((pltpu.roll(only_r, k * half, axis=1) for k in range(1, LANES // half)), only_r)

    cos, sin = jnp.cos(ang), jnp.sin(ang)
    for r in range(LANES // half):
        cos_ref[r] = spread(cos, r)
        sin_ref[r] = spread(sin, r)


def _rope_tables(positions):
    t = positions.size
    half = MLA_ROPE_DIM // 2
    per_row = LANES // half
    rows = t // per_row
    inv_freq = ROPE_THETA ** (-jnp.arange(half, dtype=F32) / half)
    freq_row = jnp.tile(inv_freq, per_row)[None, :]
    pos_rows = jnp.repeat(positions.reshape(per_row, rows).T, half, axis=1)
    tm = 1024
    cos, sin = pl.pallas_call(
        _rope_table_kernel,
        grid=(rows // tm,),
        in_specs=[pl.BlockSpec((tm, LANES), lambda i: (i, 0)), _resident((1, LANES))],
        out_specs=[pl.BlockSpec((per_row, tm, LANES), lambda i: (0, i, 0))] * 2,
        out_shape=[jax.ShapeDtypeStruct((per_row, rows, LANES), F32)] * 2,
        compiler_params=_params(),
        name="rope_tables",
    )(pos_rows, freq_row)
    return cos.reshape(t, LANES), sin.reshape(t, LANES)


def _pool_mixer(ax, ag, seq_tile, halo_ref, w_ref, sc_ref, o_ref):
    tm = ax.shape[0]

    @pl.when(seq_tile == 0)
    def _():
        halo_ref[...] = jnp.zeros_like(halo_ref)

    halo = halo_ref[...]
    head = POOL_HALO
    t_head = seq_tile * tm + lax.broadcasted_iota(jnp.int32, (head, POOL_GROUP_DIM), 0)
    for gi, win in enumerate(POOL_WINDOWS):
        cols = slice(gi * POOL_GROUP_DIM, (gi + 1) * POOL_GROUP_DIM)
        x = ax[:, cols]
        acc = jnp.concatenate([halo[:, cols], x], axis=0)
        span = 1
        while span < win:
            acc = acc + pltpu.roll(acc, span, axis=0)
            span *= 2
        count_head = jnp.minimum(t_head + 1, win).astype(F32)
        mean = jnp.concatenate([acc[POOL_HALO:POOL_HALO + head] / count_head,
                                acc[POOL_HALO + head:] * (1.0 / win)], axis=0)
        d = (mean - x).astype(BF16)
        y = jnp.dot(d, w_ref[gi], preferred_element_type=F32)
        o_ref[:, cols] = (y * sc_ref[:, cols] * _silu(ag[:, cols])).astype(o_ref.dtype)
    halo_ref[...] = ax[tm - POOL_HALO:, :]


def _sgu_mixer(uv, cg, ng_ref, nb_ref, ws_ref, bs_ref, o_ref):
    tm = uv.shape[0]
    uv = 0.5 * uv * (1.0 + lax.erf(uv * (2.0 ** -0.5)))
    u = uv[:, :SGU_WIDTH]
    vb = _layer_norm(uv[:, SGU_WIDTH:], ng_ref[...], nb_ref[...]).astype(BF16)
    gate = _silu(cg)
    n_chunks = tm // SGU_CHUNK
    row = lax.broadcasted_iota(jnp.int32, (SGU_CHUNK, SGU_CHUNK), 0)
    col = lax.broadcasted_iota(jnp.int32, (SGU_CHUNK, SGU_CHUNK), 1)
    for h in range(SGU_HEADS):
        cols = slice(h * SGU_HEAD_DIM, (h + 1) * SGU_HEAD_DIM)
        w = jnp.where(col <= row, ws_ref[h], 0.0).astype(BF16)
        vh = jnp.concatenate([vb[c * SGU_CHUNK:(c + 1) * SGU_CHUNK, cols] for c in range(n_chunks)], axis=1)
        mixed = jnp.dot(w, vh, preferred_element_type=F32) + bs_ref[h]
        for c in range(n_chunks):
            rows = slice(c * SGU_CHUNK, (c + 1) * SGU_CHUNK)
            piece = mixed[:, c * SGU_HEAD_DIM:(c + 1) * SGU_HEAD_DIM]
            o_ref[rows, cols] = (u[rows, cols] * piece * gate[rows, cols]).astype(o_ref.dtype)


def _front_kernel(x_ref, g_ref, b_ref, wt_ref, pw_ref, psc_ref, ng_ref, nb_ref,
                  ws_ref, bs_ref, lat_ref, bg_ref, ya_ref, yc_ref, *rest, apply_ln):
    x = x_ref[...]
    if apply_ln:
        h_ref, halo_ref = rest
        x = _layer_norm(x, g_ref[...], b_ref[...])
        h_ref[...] = x
    else:
        (halo_ref,) = rest
    xb = x.astype(BF16)

    def proj(lo, hi):
        return lax.dot_general(xb, wt_ref[lo:hi, :], (((1,), (1,)), ((), ())), preferred_element_type=F32)

    uv, cg = proj(SRC_UV, SRC_CG), proj(SRC_CG, SRC_END)
    ax, ag = proj(SRC_AX, SRC_AG), proj(SRC_AG, SRC_CQ)
    n_lat = SRC_BG - SRC_CQ
    lat_ref[:, :n_lat] = proj(SRC_CQ, SRC_BG).astype(lat_ref.dtype)
    lat_ref[:, n_lat:] = jnp.zeros((xb.shape[0], LAT_COLS - n_lat), lat_ref.dtype)
    _sgu_mixer(uv, cg, ng_ref, nb_ref, ws_ref, bs_ref, yc_ref)
    _pool_mixer(ax, ag, pl.program_id(1), halo_ref, pw_ref, psc_ref, ya_ref)
    bg_ref[...] = proj(SRC_BG, SRC_UV).astype(bg_ref.dtype)


def _front(x3d, ln_g, ln_b, w_in_t, w_pool, pool_scale, sgu_g, sgu_b, w_s, b_s_col, l, *, apply_ln):
    b, s, _ = x3d.shape
    assert w_in_t.shape[1:] == (SRC_END, D_MODEL)
    assert TM_FRONT % SGU_CHUNK == 0 and POOL_HALO >= max(POOL_WINDOWS) - 1
    stacks = (w_in_t, w_pool, pool_scale, sgu_g, sgu_b, w_s, b_s_col)

    def rows(width):
        return pl.BlockSpec((None, TM_FRONT, width), lambda i, j: (i, j, 0))

    def out(width):
        return jax.ShapeDtypeStruct((b, s, width), BF16)

    return pl.pallas_call(
        functools.partial(_front_kernel, apply_ln=apply_ln),
        grid=(b, s // TM_FRONT),
        in_specs=[rows(D_MODEL), _resident((1, D_MODEL)), _resident((1, D_MODEL))]
                 + [_layer(p, l) for p in stacks],
        out_specs=[rows(LAT_COLS), rows(MLA_WIDTH), rows(POOL_WIDTH), rows(SGU_WIDTH)]
                  + ([rows(D_MODEL)] if apply_ln else []),
        out_shape=[out(LAT_COLS), out(MLA_WIDTH), out(POOL_WIDTH), out(SGU_WIDTH)]
                  + ([jax.ShapeDtypeStruct((b, s, D_MODEL), F32)] if apply_ln else []),
        scratch_shapes=[pltpu.VMEM((POOL_HALO, POOL_WIDTH), F32)],
        compiler_params=pltpu.CompilerParams(vmem_limit_bytes=VMEM_LIMIT,
                                             dimension_semantics=("arbitrary", "arbitrary")),
        name="front",
    )(x3d, ln_g, ln_b, *stacks)


def _project_qkv(cq_ref, ckv_ref, kr_ref, cos_ref, sin_ref, qg_ref, wqn_ref, wqr_ref,
                     kvg_ref, wk_ref, wvt_ref, qn_ref, qr_ref, kn_ref, kr_out_ref, vt_ref):
    tm = cq_ref.shape[0]
    scale = MLA_QK_DIM ** -0.5 * math.log2(math.e)
    cos = cos_ref[...]
    sin = sin_ref[...]
    lane = lax.broadcasted_iota(jnp.int32, (tm, LANES), 1)
    first_half = (lane % MLA_ROPE_DIM) < (MLA_ROPE_DIM // 2)
    low_head = lane < MLA_ROPE_DIM

    def rope(x):
        rotated = jnp.where(first_half,
                            -pltpu.roll(x, LANES - MLA_ROPE_DIM // 2, axis=1),
                            pltpu.roll(x, MLA_ROPE_DIM // 2, axis=1))
        return x * cos + rotated * sin

    cqn = _rms_norm(cq_ref[...].astype(F32), qg_ref[...]).astype(BF16)
    qn = jnp.dot(cqn, wqn_ref[...], preferred_element_type=F32) * scale
    qr = jnp.dot(cqn, wqr_ref[...], preferred_element_type=F32)
    for pair in range(MLA_HEADS // 2):
        qr_ref[0, pair] = (rope(qr[:, pair * LANES:(pair + 1) * LANES]) * scale).astype(qr_ref.dtype)
    for h in range(MLA_HEADS):
        qn_ref[0, h] = qn[:, h * MLA_NOPE_DIM:(h + 1) * MLA_NOPE_DIM].astype(qn_ref.dtype)

    ckvn = _rms_norm(ckv_ref[...].astype(F32), kvg_ref[...]).astype(BF16)
    kn = jnp.dot(ckvn, wk_ref[...], preferred_element_type=F32)
    vt = lax.dot_general(wvt_ref[...], ckvn, (((1,), (1,)), ((), ())), preferred_element_type=F32)
    kr = rope(kr_ref[...].astype(F32))
    kr_out_ref[0] = jnp.where(low_head, kr, pltpu.roll(kr, MLA_ROPE_DIM, axis=1)).astype(kr_out_ref.dtype)
    ones = jnp.ones((VT_ROWS - MLA_V_DIM, tm), vt_ref.dtype)
    for h in range(MLA_HEADS):
        kn_ref[0, h] = kn[:, h * MLA_NOPE_DIM:(h + 1) * MLA_NOPE_DIM].astype(kn_ref.dtype)
        vt_ref[0, h, :MLA_V_DIM] = vt[h * MLA_V_DIM:(h + 1) * MLA_V_DIM, :].astype(vt_ref.dtype)
        vt_ref[0, h, MLA_V_DIM:] = ones


def _attn_kernel(cq_ref, ckv_ref, kr_in_ref, cos_ref, sin_ref, qg_ref, wqn_ref, wqr_ref, kvg_ref, wk_ref, wvt_ref,
                 o_ref, qn_ref, qr_ref, kn_ref, kr_ref, vt_ref, *bufs):
    qi = pl.program_id(1)
    n_heads = qn_ref.shape[1]
    n_chunks = kn_ref.shape[2] // TK
    hk = TK // 2
    nt_dims = (((1,), (1,)), ((), ()))
    key = lax.broadcasted_iota(jnp.int32, (hk, TQ), 0)
    qry = lax.broadcasted_iota(jnp.int32, (hk, TQ), 1)
    causal = key <= qry

    low_head = lax.broadcasted_iota(jnp.int32, (TQ, LANES), 1) < MLA_ROPE_DIM

    def col_max(s):
        return jnp.max(s, axis=0, keepdims=True)

    def q_rows(h):
        keep = low_head if h % 2 == 0 else jnp.logical_not(low_head)
        rot = jnp.where(keep, qr_ref[0, h // 2], jnp.zeros((), qr_ref.dtype))
        return jnp.concatenate([qn_ref[0, h], rot], axis=1)

    def k_rows(h, lo, hi):
        return jnp.concatenate([kn_ref[0, h, lo:hi, :], kr_ref[0, lo:hi, :]], axis=1)

    def produce(h, s_ref, nk):
        q = q_rows(h)
        s = lax.dot_general(k_rows(h, 0, nk - hk), q, nt_dims, preferred_element_type=F32)
        s_low = lax.dot_general(k_rows(h, nk - hk, nk), q[hk:], nt_dims, preferred_element_type=F32)
        top = jnp.where(causal, s[nk - TK:], -jnp.inf)
        low = jnp.where(causal[:, :hk], s_low, -jnp.inf)
        s_ref[nk - TK:nk - hk] = top
        s_ref[nk - hk:nk, hk:] = low
        m = col_max(top)
        if nk > TK:
            s_ref[:nk - TK] = s[:nk - TK]
            m = jnp.maximum(m, col_max(s[:nk - TK]))
        return jnp.concatenate([m[:, :hk], jnp.maximum(m[:, hk:], col_max(low))], axis=1)

    def consume(h, s_ref, m, nk):
        pb = jnp.exp2(s_ref[:nk - hk] - m).astype(BF16)
        p_low = jnp.exp2(s_ref[nk - hk:nk, hk:] - m[:, hk:]).astype(BF16)
        o_first = jnp.dot(vt_ref[0, h, :, :nk - hk], pb[:, :hk], preferred_element_type=F32)
        o_last = jnp.dot(vt_ref[0, h, :, :nk], jnp.concatenate([pb[:, hk:], p_low], axis=0),
                         preferred_element_type=F32)
        o = jnp.concatenate([o_first, o_last], axis=1)
        o = o[:MLA_V_DIM] / o[MLA_V_DIM:MLA_V_DIM + 1]
        o_ref[0, h] = o.T.astype(o_ref.dtype)

    def run(nk):
        new = pl.ds(nk - TK, TK)
        _project_qkv(cq_ref, ckv_ref, kr_in_ref, cos_ref, sin_ref, qg_ref, wqn_ref, wqr_ref, kvg_ref, wk_ref,
                     wvt_ref, qn_ref, qr_ref, kn_ref.at[:, :, new, :], kr_ref.at[:, new, :],
                     vt_ref.at[:, :, :, new])
        ahead = len(bufs) - 1
        maxima = [produce(h, bufs[h % len(bufs)], nk) for h in range(min(ahead, n_heads))]
        for h in range(n_heads):
            if h + ahead < n_heads:
                maxima.append(produce(h + ahead, bufs[(h + ahead) % len(bufs)], nk))
            consume(h, bufs[h % len(bufs)], maxima[h], nk)

    for c in range(n_chunks):
        pl.when(qi == c)(functools.partial(run, (c + 1) * TK))


def _attention(lat3d, cos, sin, q_norm_g, wq_nope, wq_rope, kv_norm_g, wk_nope, wv_t, l):
    b, s, _ = lat3d.shape
    h = MLA_HEADS
    nb = s // TQ
    assert TQ == TK and h % 2 == 0
    stacks = (q_norm_g, wq_nope, wq_rope, kv_norm_g, wk_nope, wv_t)
    tab = pl.BlockSpec((TQ, LANES), lambda i, j: (i * nb + j, 0))

    def seg(width, col):
        return pl.BlockSpec((None, TQ, width), lambda i, j: (i, j, col // width))

    return pl.pallas_call(
        _attn_kernel,
        grid=(b, nb),
        in_specs=[seg(MLA_Q_RANK, LAT_CQ), seg(MLA_KV_RANK, LAT_CKV), seg(LANES, LAT_KR), tab, tab]
                 + [_layer(p, l) for p in stacks],
        out_specs=pl.BlockSpec((1, h, TQ, MLA_V_DIM), lambda i, j: (i, 0, j, 0)),
        out_shape=jax.ShapeDtypeStruct((b, h, s, MLA_V_DIM), BF16),
        scratch_shapes=[pltpu.VMEM((1, h, TQ, LANES), BF16), pltpu.VMEM((1, h // 2, TQ, LANES), BF16),
                        pltpu.VMEM((1, h, s, LANES), BF16), pltpu.VMEM((1, s, LANES), BF16),
                        pltpu.VMEM((1, h, VT_ROWS, s), BF16)]
                       + [pltpu.VMEM((s, TQ), F32)] * N_SCORE_BUFS,
        compiler_params=pltpu.CompilerParams(vmem_limit_bytes=VMEM_LIMIT,
                                             dimension_semantics=("arbitrary", "arbitrary")),
        name="mla_attention",
    )(lat3d, lat3d, lat3d, cos, sin, *stacks)


def _out_proj_kernel(ya_ref, yb_ref, bg_ref, yc_ref, h_ref, w_ref, bo_ref, pg_ref, pb_ref, o_ref):
    attn = jnp.concatenate([yb_ref[0, h] for h in range(MLA_HEADS)], axis=1).astype(F32)
    yb = (attn * _silu(bg_ref[...].astype(F32))).astype(BF16)
    y = jnp.concatenate([ya_ref[...], yb, yc_ref[...]], axis=1)
    out = jnp.dot(y, w_ref[...], preferred_element_type=F32) + bo_ref[...]
    o_ref[...] = _layer_norm(ALPHA * h_ref[...] + out, pg_ref[...], pb_ref[...])


def _out_proj(y_a, y_b, b_gate, y_c, h3d, w_out, b_out, post_g, post_b, l):
    b, s, _ = h3d.shape

    def rows(width):
        return pl.BlockSpec((None, TM_OUT, width), lambda i, j: (i, j, 0))

    return pl.pallas_call(
        _out_proj_kernel,
        grid=(b, s // TM_OUT),
        in_specs=[rows(POOL_WIDTH),
                  pl.BlockSpec((1, MLA_HEADS, TM_OUT, MLA_V_DIM), lambda i, j: (i, 0, j, 0)),
                  rows(MLA_WIDTH), rows(SGU_WIDTH), rows(D_MODEL),
                  _layer(w_out, l), _layer(b_out, l), _layer(post_g, l), _layer(post_b, l)],
        out_specs=rows(D_MODEL),
        out_shape=jax.ShapeDtypeStruct((b, s, D_MODEL), F32),
        compiler_params=_params(),
        name="out_proj",
    )(y_a, y_b, b_gate, y_c, h3d, w_out, b_out, post_g, post_b)


def kernel(x, positions, ln_in_g, ln_in_b, w_in, pool_w, pool_scale, q_norm_g, w_uq, kv_norm_g, w_ukv,
           sgu_norm_g, sgu_norm_b, sgu_w, sgu_b, w_out, b_out, ln_post_g, ln_post_b):
    b, s, d = x.shape
    assert d == D_MODEL and s % TQ == 0 and s % TM_OUT == 0 and s % TM_FRONT == 0

    cos, sin = _rope_tables(positions)
    ln_in_g2, ln_in_b2 = ln_in_g[None, :], ln_in_b[None, :]

    depth = w_in.shape[0]
    assert depth == DEPTH
    w_in_t = jnp.swapaxes(w_in, 1, 2).astype(BF16)
    wq = w_uq.reshape(depth, MLA_Q_RANK, MLA_HEADS, MLA_QK_DIM)
    wq_nope = wq[..., :MLA_NOPE_DIM].reshape(depth, MLA_Q_RANK, MLA_HEADS * MLA_NOPE_DIM).astype(BF16)
    wq_rope = wq[..., MLA_NOPE_DIM:].reshape(depth, MLA_Q_RANK, MLA_HEADS * MLA_ROPE_DIM).astype(BF16)
    wkv = w_ukv.reshape(depth, MLA_KV_RANK, MLA_HEADS, MLA_NOPE_DIM + MLA_V_DIM)
    wk_nope = wkv[..., :MLA_NOPE_DIM].reshape(depth, MLA_KV_RANK, MLA_HEADS * MLA_NOPE_DIM).astype(BF16)
    wv_t = wkv[..., MLA_NOPE_DIM:].reshape(depth, MLA_KV_RANK, MLA_HEADS * MLA_V_DIM).swapaxes(1, 2).astype(BF16)
    w_pool = pool_w.astype(BF16)
    w_out_b = w_out.astype(BF16)

    def row(p):
        return p[:, None, :]

    h = x
    for l in range(depth):
        first = l == 0
        lat, b_gate, y_a, y_c, *normed = _front(
            h, ln_in_g2, ln_in_b2, w_in_t, w_pool, row(pool_scale),
            row(sgu_norm_g), row(sgu_norm_b), sgu_w, sgu_b[..., None], l, apply_ln=first)
        if first:
            (h,) = normed
        y_b = _attention(lat, cos, sin, row(q_norm_g), wq_nope, wq_rope, row(kv_norm_g), wk_nope, wv_t, l)
        h = _out_proj(y_a, y_b, b_gate, y_c, h, w_out_b, row(b_out), row(ln_post_g), row(ln_post_b), l)
    return h
```
